```python
import jax, jax.numpy as jnp
from jax import lax
import numpy as np

D_MODEL = 2048
BATCH = 2
SEQ = 4096
DEPTH = 4
DEC_BATCH = 32
DEC_SEQ = 16
PAST_LEN = 4096

CHUNK = 64
N_MIXERS = 4
EPS = 1e-6
POOL_WINDOWS = (2, 4, 8, 16)
N_POOL_GROUPS = len(POOL_WINDOWS)
POOL_GROUP = D_MODEL // N_POOL_GROUPS
POOL_HIST = max(POOL_WINDOWS) - 1
GMLP_CHUNK = 128
GMLP_HEADS = 8
GMLP_WIDTH = D_MODEL
GMLP_HEAD_DIM = GMLP_WIDTH // GMLP_HEADS
SCONV_WIDTH = 3
SCONV_HIST = SCONV_WIDTH - 1
CCONV_WIDTH = 31
CCONV_HIST = CCONV_WIDTH - 1
D_FF = -(-8 * D_MODEL // (3 * 256)) * 256

kernel_name = "streaming_pool_gmlp_shortconv_conformer_trunk"


def rmsnorm(x, g):
    xf = x.astype(jnp.float32)
    y = xf * lax.rsqrt(jnp.mean(xf * xf, axis=-1, keepdims=True) + EPS)
    return (y * g.astype(jnp.float32)).astype(x.dtype)


def layernorm(x, g, b):
    xf = x.astype(jnp.float32)
    mu = jnp.mean(xf, axis=-1, keepdims=True)
    xc = xf - mu
    var = jnp.mean(xc * xc, axis=-1, keepdims=True)
    y = xc * lax.rsqrt(var + EPS) * g.astype(jnp.float32) + b.astype(jnp.float32)
    return y.astype(x.dtype)


def causal_depthwise_conv(x, hist, w):
    K, C = w.shape
    xe = jnp.concatenate([hist.astype(x.dtype), x], axis=1)
    y = lax.conv_general_dilated(xe, w.astype(x.dtype)[:, None, :], window_strides=(1,), padding='VALID',
                                 dimension_numbers=('NWC', 'WIO', 'NWC'), feature_group_count=C)
    return y, xe[:, -(K - 1):, :]


def pool_mixer(h, hist, pos0, pool_w, pool_scale):
    B, S, _ = h.shape
    xe = jnp.concatenate([hist.astype(h.dtype), h], axis=1)
    xf = xe.astype(jnp.float32)
    cs = jnp.concatenate([jnp.zeros_like(xf[:, :1]), jnp.cumsum(xf, axis=1)], axis=1)
    end = cs[:, POOL_HIST + 1:]
    pos = pos0 + jnp.arange(S)
    pooled = []
    for g, w in enumerate(POOL_WINDOWS):
        sl = slice(g * POOL_GROUP, (g + 1) * POOL_GROUP)
        start = cs[:, POOL_HIST + 1 - w: POOL_HIST + 1 - w + S, sl]
        cnt = jnp.minimum(pos + 1, w).astype(jnp.float32)[None, :, None]
        pooled.append((end[..., sl] - start) / cnt)
    pooled = jnp.concatenate(pooled, axis=-1)
    diff = (pooled - h.astype(jnp.float32)).astype(h.dtype).reshape(B, S, N_POOL_GROUPS, POOL_GROUP)
    y = jnp.einsum('bsgc,gcd->bsgd', diff, pool_w).reshape(B, S, D_MODEL)
    return y * pool_scale, xe[:, -POOL_HIST:, :]


def gmlp_mixer(h, w_in, b_in, ln_g, ln_b, w_s, b_s, w_out):
    B, S, _ = h.shape
    z = jax.nn.gelu(h @ w_in + b_in)
    u, v = jnp.split(z, 2, axis=-1)
    v = layernorm(v, ln_g, ln_b)
    n_chunks = -(-S // GMLP_CHUNK)
    pad = n_chunks * GMLP_CHUNK - S
    idx = jnp.arange(GMLP_CHUNK)
    mask = (idx[None, :] // CHUNK) <= (idx[:, None] // CHUNK)
    ws = jnp.where(mask[None], w_s, jnp.zeros((), w_s.dtype))
    vp = jnp.pad(v, ((0, 0), (0, pad), (0, 0))).reshape(B, n_chunks, GMLP_CHUNK, GMLP_HEADS, GMLP_HEAD_DIM)
    s = jnp.einsum('hij,bcjhd->bcihd', ws, vp) + b_s.T[None, None, :, :, None]
    s = s.reshape(B, n_chunks * GMLP_CHUNK, GMLP_WIDTH)[:, :S]
    return (u * s) @ w_out, v


def short_conv_mixer(h, hist, w_in, conv_w, w_out):
    b_gate, c_gate, xin = jnp.split(h @ w_in, 3, axis=-1)
    conv, new_hist = causal_depthwise_conv(c_gate * xin, hist, conv_w)
    return (b_gate * conv) @ w_out, new_hist


def conformer_conv_mixer(h, hist, w_pw1, b_pw1, dw_w, dw_b, ln_g, ln_b, w_pw2, b_pw2):
    a, g = jnp.split(h @ w_pw1 + b_pw1, 2, axis=-1)
    glu = a * jax.nn.sigmoid(g)
    conv, new_hist = causal_depthwise_conv(glu, hist, dw_w)
    z = jax.nn.silu(layernorm(conv + dw_b, ln_g, ln_b))
    return z @ w_pw2 + b_pw2, new_hist


def swiglu(h, w_gate, w_up, w_down):
    return (jax.nn.silu(h @ w_gate) * (h @ w_up)) @ w_down


def trunk(x, pool_hist, sconv_hist, cconv_hist, pos0, p):
    gmlp_v = None
    for i in range(DEPTH):
        h = rmsnorm(x, p['norm_mix_g'][i])
        m = i % N_MIXERS
        if m == 0:
            y, pool_hist = pool_mixer(h, pool_hist, pos0, p['pool_w'], p['pool_scale'])
        elif m == 1:
            y, gmlp_v = gmlp_mixer(h, p['gmlp_w_in'], p['gmlp_b_in'], p['gmlp_ln_g'], p['gmlp_ln_b'],
                                   p['gmlp_w_s'], p['gmlp_b_s'], p['gmlp_w_out'])
        elif m == 2:
            y, sconv_hist = short_conv_mixer(h, sconv_hist, p['sconv_w_in'], p['sconv_conv_w'], p['sconv_w_out'])
        else:
            y, cconv_hist = conformer_conv_mixer(h, cconv_hist, p['cconv_w_pw1'], p['cconv_b_pw1'],
                                                 p['cconv_dw_w'], p['cconv_dw_b'], p['cconv_ln_g'],
                                                 p['cconv_ln_b'], p['cconv_w_pw2'], p['cconv_b_pw2'])
        x = x + y
        h = rmsnorm(x, p['norm_ffn_g'][i])
        x = x + swiglu(h, p['ffn_w_gate'][i], p['ffn_w_up'][i], p['ffn_w_down'][i])
    return rmsnorm(x, p['norm_final_g']), pool_hist, gmlp_v, sconv_hist, cconv_hist


def setup_inputs(seed: int = 0) -> dict:
    key = jax.random.key(seed)
    ks = jax.random.split(key, 40)
    f32 = jnp.float32

    def nrm(k, shape, fan_in):
        return jax.random.normal(k, shape, f32) * (fan_in ** -0.5)

    def gain(k, shape):
        return 1.0 + 0.02 * jax.random.normal(k, shape, f32)

    def bias(k, shape):
        return 0.02 * jax.random.normal(k, shape, f32)

    D = D_MODEL
    return {
        'x_prompt': jax.random.normal(ks[0], (BATCH, SEQ, D), f32),
        'x_sample': jax.random.normal(ks[1], (DEC_BATCH, DEC_SEQ, D), f32),
        'state_pool': jax.random.normal(ks[2], (DEC_BATCH, POOL_HIST, D), f32),
        'state_sconv': jax.random.normal(ks[3], (DEC_BATCH, SCONV_HIST, D), f32),
        'state_cconv': 0.5 * jax.random.normal(ks[4], (DEC_BATCH, CCONV_HIST, D), f32),
        'norm_mix_g': gain(ks[5], (DEPTH, D)),
        'norm_ffn_g': gain(ks[6], (DEPTH, D)),
        'norm_final_g': gain(ks[7], (D,)),
        'pool_w': nrm(ks[8], (N_POOL_GROUPS, POOL_GROUP, POOL_GROUP), POOL_GROUP),
        'pool_scale': gain(ks[9], (D,)),
        'gmlp_w_in': nrm(ks[10], (D, 2 * GMLP_WIDTH), D),
        'gmlp_b_in': bias(ks[11], (2 * GMLP_WIDTH,)),
        'gmlp_ln_g': gain(ks[12], (GMLP_WIDTH,)),
        'gmlp_ln_b': bias(ks[13], (GMLP_WIDTH,)),
        'gmlp_w_s': nrm(ks[14], (GMLP_HEADS, GMLP_CHUNK, GMLP_CHUNK), GMLP_CHUNK),
        'gmlp_b_s': gain(ks[15], (GMLP_HEADS, GMLP_CHUNK)),
        'gmlp_w_out': nrm(ks[16], (GMLP_WIDTH, D), GMLP_WIDTH),
        'sconv_w_in': nrm(ks[17], (D, 3 * D), D),
        'sconv_conv_w': nrm(ks[18], (SCONV_WIDTH, D), SCONV_WIDTH),
        'sconv_w_out': nrm(ks[19], (D, D), D),
        'cconv_w_pw1': nrm(ks[20], (D, 2 * D), D),
        'cconv_b_pw1': bias(ks[21], (2 * D,)),
        'cconv_dw_w': nrm(ks[22], (CCONV_WIDTH, D), CCONV_WIDTH),
        'cconv_dw_b': bias(ks[23], (D,)),
        'cconv_ln_g': gain(ks[24], (D,)),
        'cconv_ln_b': bias(ks[25], (D,)),
        'cconv_w_pw2': nrm(ks[26], (D, D), D),
        'cconv_b_pw2': bias(ks[27], (D,)),
        'ffn_w_gate': nrm(ks[28], (DEPTH, D, D_FF), D),
        'ffn_w_up': nrm(ks[29], (DEPTH, D, D_FF), D),
        'ffn_w_down': nrm(ks[30], (DEPTH, D_FF, D), D_FF),
    }


def reference(x_prompt, x_sample, state_pool, state_sconv, state_cconv,
              norm_mix_g, norm_ffn_g, norm_final_g,
              pool_w, pool_scale,
              gmlp_w_in, gmlp_b_in, gmlp_ln_g, gmlp_ln_b, gmlp_w_s, gmlp_b_s, gmlp_w_out,
              sconv_w_in, sconv_conv_w, sconv_w_out,
              cconv_w_pw1, cconv_b_pw1, cconv_dw_w, cconv_dw_b, cconv_ln_g, cconv_ln_b, cconv_w_pw2, cconv_b_pw2,
              ffn_w_gate, ffn_w_up, ffn_w_down):
    p = dict(norm_mix_g=norm_mix_g, norm_ffn_g=norm_ffn_g, norm_final_g=norm_final_g,
             pool_w=pool_w, pool_scale=pool_scale,
             gmlp_w_in=gmlp_w_in, gmlp_b_in=gmlp_b_in, gmlp_ln_g=gmlp_ln_g, gmlp_ln_b=gmlp_ln_b,
             gmlp_w_s=gmlp_w_s, gmlp_b_s=gmlp_b_s, gmlp_w_out=gmlp_w_out,
             sconv_w_in=sconv_w_in, sconv_conv_w=sconv_conv_w, sconv_w_out=sconv_w_out,
             cconv_w_pw1=cconv_w_pw1, cconv_b_pw1=cconv_b_pw1, cconv_dw_w=cconv_dw_w, cconv_dw_b=cconv_dw_b,
             cconv_ln_g=cconv_ln_g, cconv_ln_b=cconv_ln_b, cconv_w_pw2=cconv_w_pw2, cconv_b_pw2=cconv_b_pw2,
             ffn_w_gate=ffn_w_gate, ffn_w_up=ffn_w_up, ffn_w_down=ffn_w_down)
    bp = x_prompt.shape[0]
    dt = x_prompt.dtype
    y_prompt, pool_p, _, sconv_p, cconv_p = trunk(
        x_prompt,
        jnp.zeros((bp, POOL_HIST, D_MODEL), dt),
        jnp.zeros((bp, SCONV_HIST, D_MODEL), dt),
        jnp.zeros((bp, CCONV_HIST, D_MODEL), dt),
        0, p)
    y_sample, pool_s, gmlp_v_s, sconv_s, cconv_s = trunk(
        x_sample, state_pool, state_sconv, state_cconv, PAST_LEN, p)
    return (y_prompt, y_sample, pool_p, pool_s, gmlp_v_s, sconv_p, sconv_s, cconv_p, cconv_s)
```

```python
import functools

import jax
import jax.numpy as jnp
from jax import lax
from jax.experimental import pallas as pl
from jax.experimental.pallas import tpu as pltpu

F32 = jnp.float32
BF16 = jnp.bfloat16

EPS = 1e-6
PAST_LEN = 4096
POOL_WINDOWS = (2, 4, 8, 16)
POOL_HIST = max(POOL_WINDOWS) - 1
POOL_HALO = 16
GMLP_CHUNK = 128
GMLP_CAUSAL_SHIFT = 6
GMLP_HEADS = 8
SCONV_WIDTH = 3
SCONV_HALO = 8
CCONV_WIDTH = 31
CCONV_HALO = 32

TOKEN_TILE = 512
COL_TILE = 512
VMEM_LIMIT_BYTES = 56 * 1024 * 1024


def _rms(x, g):
    ms = jnp.mean(x * x, axis=-1, keepdims=True)
    return x * lax.rsqrt(ms + EPS) * g


def _layernorm(x, g, b):
    mu = jnp.mean(x, axis=-1, keepdims=True)
    xc = x - mu
    var = jnp.mean(xc * xc, axis=-1, keepdims=True)
    return xc * lax.rsqrt(var + EPS) * g + b


def _dot(a, b):
    return jnp.dot(a, b, preferred_element_type=F32)


def _params(n_axes):
    return pltpu.CompilerParams(
        dimension_semantics=("arbitrary",) * n_axes,
        vmem_limit_bytes=VMEM_LIMIT_BYTES)


def _ffn_kernel(x_ref, g_ref, wg_ref, wu_ref, wd_ref, *rest, final_norm):
    if final_norm:
        gf_ref, o_ref, h_ref = rest
    else:
        o_ref, h_ref = rest
    k = pl.program_id(1)

    @pl.when(k == 0)
    def _():
        x = x_ref[...]
        h_ref[...] = _rms(x, g_ref[...]).astype(BF16)
        o_ref[...] = x

    h = h_ref[...]
    gate = _dot(h, wg_ref[...])
    up = _dot(h, wu_ref[...])
    act = (gate * jax.nn.sigmoid(gate) * up).astype(BF16)
    o_ref[...] += _dot(act, wd_ref[...])

    if final_norm:
        @pl.when(k == pl.num_programs(1) - 1)
        def _():
            o_ref[...] = _rms(o_ref[...], gf_ref[...])


def _ffn(x, g, wg, wu, wd, g_final=None):
    m, d = x.shape
    f = wg.shape[1]
    tm = min(TOKEN_TILE, m)
    tf = COL_TILE
    final_norm = g_final is not None
    in_specs = [
        pl.BlockSpec((tm, d), lambda i, k: (i, 0)),
        pl.BlockSpec((1, d), lambda i, k: (0, 0)),
        pl.BlockSpec((d, tf), lambda i, k: (0, k)),
        pl.BlockSpec((d, tf), lambda i, k: (0, k)),
        pl.BlockSpec((tf, d), lambda i, k: (k, 0)),
    ]
    args = [x, g, wg, wu, wd]
    if final_norm:
        in_specs.append(pl.BlockSpec((1, d), lambda i, k: (0, 0)))
        args.append(g_final)
    return pl.pallas_call(
        functools.partial(_ffn_kernel, final_norm=final_norm),
        grid=(m // tm, f // tf),
        in_specs=in_specs,
        out_specs=pl.BlockSpec((tm, d), lambda i, k: (i, 0)),
        out_shape=jax.ShapeDtypeStruct((m, d), F32),
        scratch_shapes=[pltpu.VMEM((tm, d), BF16)],
        compiler_params=_params(2),
        name="ffn_final" if final_norm else "ffn",
    )(*args)


def _pool_body(x_ref, hist, g_ref, pw_ref, sc_ref, o_ref, xe_ref, pos_first):
    bt, l, d = x_ref.shape
    grp = d // len(POOL_WINDOWS)
    x = x_ref[...]
    xe_ref[:, 0:POOL_HALO, :] = hist
    xe_ref[:, POOL_HALO:, :] = _rms(x, g_ref[...])
    pos = lax.broadcasted_iota(jnp.int32, (1, l, 1), 1) + pos_first
    for gi, w in enumerate(POOL_WINDOWS):
        c0 = gi * grp
        hg = xe_ref[:, POOL_HALO:, c0:c0 + grp]
        acc = hg
        for k in range(1, w):
            acc = acc + xe_ref[:, POOL_HALO - k:POOL_HALO - k + l, c0:c0 + grp]
        inv_cnt = 1.0 / jnp.minimum(pos + 1, w).astype(F32)
        diff = (acc * inv_cnt - hg).reshape(bt * l, grp).astype(BF16)
        y = _dot(diff, pw_ref[gi]) * sc_ref[:, c0:c0 + grp]
        o_ref[:, :, c0:c0 + grp] = x_ref[:, :, c0:c0 + grp] + y.reshape(bt, l, grp)


def _pool_prompt_kernel(x_ref, g_ref, pw_ref, sc_ref, o_ref, st_ref, xe_ref, carry_ref, *, pos0):
    i = pl.program_id(1)
    l = x_ref.shape[1]

    @pl.when(i == 0)
    def _():
        carry_ref[...] = jnp.zeros_like(carry_ref)

    _pool_body(x_ref, carry_ref[...], g_ref, pw_ref, sc_ref, o_ref, xe_ref, pos0 + i * l)
    carry_ref[...] = xe_ref[:, l:l + POOL_HALO, :]
    st_ref[...] = xe_ref[:, l + POOL_HALO - POOL_HIST:l + POOL_HALO, :]


def _pool_sample_kernel(x_ref, hist_ref, g_ref, pw_ref, sc_ref, o_ref, st_ref, xe_ref, *, pos0):
    l = x_ref.shape[1]
    _pool_body(x_ref, hist_ref[...], g_ref, pw_ref, sc_ref, o_ref, xe_ref, pos0)
    st_ref[...] = xe_ref[:, l + POOL_HALO - POOL_HIST:l + POOL_HALO, :]


def _pool_prompt(x, g, pw, scale, pos0):
    b, s, d = x.shape
    ts = TOKEN_TILE
    ng = len(POOL_WINDOWS)
    return pl.pallas_call(
        functools.partial(_pool_prompt_kernel, pos0=pos0),
        grid=(b, s // ts),
        in_specs=[
            pl.BlockSpec((1, ts, d), lambda bi, i: (bi, i, 0)),
            pl.BlockSpec((1, d), lambda bi, i: (0, 0)),
            pl.BlockSpec((ng, d // ng, d // ng), lambda bi, i: (0, 0, 0)),
            pl.BlockSpec((1, d), lambda bi, i: (0, 0)),
        ],
        out_specs=[
            pl.BlockSpec((1, ts, d), lambda bi, i: (bi, i, 0)),
            pl.BlockSpec((1, POOL_HIST, d), lambda bi, i: (bi, 0, 0)),
        ],
        out_shape=[jax.ShapeDtypeStruct((b, s, d), F32),
                   jax.ShapeDtypeStruct((b, POOL_HIST, d), F32)],
        scratch_shapes=[pltpu.VMEM((1, ts + POOL_HALO, d), F32),
                        pltpu.VMEM((1, POOL_HALO, d), F32)],
        compiler_params=_params(2),
        name="pool_prompt",
    )(x, g, pw, scale)


def _pool_sample(x, hist16, g, pw, scale, pos0):
    b, s, d = x.shape
    bt = 8
    ng = len(POOL_WINDOWS)
    return pl.pallas_call(
        functools.partial(_pool_sample_kernel, pos0=pos0),
        grid=(b // bt,),
        in_specs=[
            pl.BlockSpec((bt, s, d), lambda i: (i, 0, 0)),
            pl.BlockSpec((bt, POOL_HALO, d), lambda i: (i, 0, 0)),
            pl.BlockSpec((1, d), lambda i: (0, 0)),
            pl.BlockSpec((ng, d // ng, d // ng), lambda i: (0, 0, 0)),
            pl.BlockSpec((1, d), lambda i: (0, 0)),
        ],
        out_specs=[
            pl.BlockSpec((bt, s, d), lambda i: (i, 0, 0)),
            pl.BlockSpec((bt, POOL_HIST, d), lambda i: (i, 0, 0)),
        ],
        out_shape=[jax.ShapeDtypeStruct((b, s, d), F32),
                   jax.ShapeDtypeStruct((b, POOL_HIST, d), F32)],
        scratch_shapes=[pltpu.VMEM((bt, s + POOL_HALO, d), F32)],
        compiler_params=_params(1),
        name="pool_sample",
    )(x, hist16, g, pw, scale)


def _gmlp_in_kernel(x_ref, g_ref, w_ref, b_ref, lng_ref, lnb_ref, u_ref, v_ref, h_ref, vraw_ref):
    j = pl.program_id(1)
    nj = pl.num_programs(1)
    half = nj // 2

    @pl.when(j == 0)
    def _():
        h_ref[...] = _rms(x_ref[...], g_ref[...]).astype(BF16)

    z = jax.nn.gelu(_dot(h_ref[...], w_ref[...]) + b_ref[...])

    @pl.when(j < half)
    def _():
        u_ref[...] = z

    @pl.when(j >= half)
    def _():
        vraw_ref[j - half] = z

    @pl.when(j == nj - 1)
    def _():
        parts = [vraw_ref[c] for c in range(half)]
        v = _layernorm(jnp.concatenate(parts, axis=1), lng_ref[...], lnb_ref[...])
        v_ref[...] = v


def _gmlp_in(x, g, w_in, b_in, ln_g, ln_b):
    m, d = x.shape
    n = w_in.shape[1]
    width = n // 2
    tm = min(TOKEN_TILE, m)
    tn = 1024
    nj = n // tn
    half = nj // 2
    return pl.pallas_call(
        _gmlp_in_kernel,
        grid=(m // tm, nj),
        in_specs=[
            pl.BlockSpec((tm, d), lambda i, j: (i, 0)),
            pl.BlockSpec((1, d), lambda i, j: (0, 0)),
            pl.BlockSpec((d, tn), lambda i, j: (0, j)),
            pl.BlockSpec((1, tn), lambda i, j: (0, j)),
            pl.BlockSpec((1, width), lambda i, j: (0, 0)),
            pl.BlockSpec((1, width), lambda i, j: (0, 0)),
        ],
        out_specs=[
            pl.BlockSpec((tm, tn), lambda i, j: (i, jnp.minimum(j, half - 1))),
            pl.BlockSpec((tm, width), lambda i, j: (i, 0)),
        ],
        out_shape=[jax.ShapeDtypeStruct((m, width), F32),
                   jax.ShapeDtypeStruct((m, width), F32)],
        scratch_shapes=[pltpu.VMEM((tm, d), BF16),
                        pltpu.VMEM((half, tm, tn), F32)],
        compiler_params=_params(2),
        name="gmlp_in",
    )(x, g, w_in, b_in, ln_g, ln_b)


def _gmlp_out_kernel(x_ref, u_ref, v_ref, ws_ref, bs_ref, wo_ref, o_ref, *, block_diag):
    hd = pl.program_id(1)

    @pl.when(hd == 0)
    def _():
        o_ref[...] = x_ref[...]

    row = lax.broadcasted_iota(jnp.int32, (GMLP_CHUNK, GMLP_CHUNK), 0)
    col = lax.broadcasted_iota(jnp.int32, (GMLP_CHUNK, GMLP_CHUNK), 1)
    if block_diag:
        shift = block_diag.bit_length() - 1
        mask = (row >> shift) == (col >> shift)
    else:
        mask = (col >> GMLP_CAUSAL_SHIFT) <= (row >> GMLP_CAUSAL_SHIFT)
    wm = jnp.where(mask, ws_ref[0], 0.0).astype(BF16)
    bias = bs_ref[0]
    parts = []
    for c in range(u_ref.shape[0] // GMLP_CHUNK):
        rows = slice(c * GMLP_CHUNK, (c + 1) * GMLP_CHUNK)
        s = _dot(wm, v_ref[rows, :].astype(BF16)) + bias
        parts.append((u_ref[rows, :] * s).astype(BF16))
    t = jnp.concatenate(parts, axis=0)
    o_ref[...] += _dot(t, wo_ref[...])


def _gmlp_out(x, u, v, ws, bs, w_out, block_diag):
    m, d = x.shape
    width = u.shape[1]
    hdim = width // GMLP_HEADS
    tm = min(TOKEN_TILE, m)
    return pl.pallas_call(
        functools.partial(_gmlp_out_kernel, block_diag=block_diag),
        grid=(m // tm, GMLP_HEADS),
        in_specs=[
            pl.BlockSpec((tm, d), lambda i, hd: (i, 0)),
            pl.BlockSpec((tm, hdim), lambda i, hd: (i, hd)),
            pl.BlockSpec((tm, hdim), lambda i, hd: (i, hd)),
            pl.BlockSpec((1, GMLP_CHUNK, GMLP_CHUNK), lambda i, hd: (hd, 0, 0)),
            pl.BlockSpec((1, GMLP_CHUNK, 1), lambda i, hd: (hd, 0, 0)),
            pl.BlockSpec((hdim, d), lambda i, hd: (hd, 0)),
        ],
        out_specs=pl.BlockSpec((tm, d), lambda i, hd: (i, 0)),
        out_shape=jax.ShapeDtypeStruct((m, d), F32),
        compiler_params=_params(2),
        name="gmlp_out",
    )(x, u, v, ws, bs, w_out)


def _merge_col_blocks(state):
    if state.ndim == 3:
        return state
    n_seq, nb, rows, tn = state.shape
    return state.transpose(0, 2, 1, 3).reshape(n_seq, rows, nb * tn)


def _sconv_kernel(x_ref, g_ref, wb_ref, wc_ref, wx_ref, cw_ref, wo_ref, *rest, sample, tiles_per_seq):
    if sample:
        hist_ref, o_ref, st_ref, h_ref, cxe_ref = rest
    else:
        o_ref, st_ref, h_ref, cxe_ref, carry_ref = rest
    i = pl.program_id(0)
    n = pl.program_id(1)
    tm = x_ref.shape[0]
    halo = SCONV_HALO
    first_tap = halo - (SCONV_WIDTH - 1)

    @pl.when(n == 0)
    def _():
        x = x_ref[...]
        h_ref[...] = _rms(x, g_ref[...]).astype(BF16)
        o_ref[...] = x

    h = h_ref[...]
    b_gate = _dot(h, wb_ref[...])
    cx = _dot(h, wc_ref[...]) * _dot(h, wx_ref[...])
    tn = cx.shape[1]

    if sample:
        bt, l = cxe_ref.shape[0], cxe_ref.shape[1] - halo
        cxe_ref[:, 0:halo, :] = hist_ref[...]
        cxe_ref[:, halo:, :] = cx.reshape(bt, l, tn)
        conv = None
        for k in range(SCONV_WIDTH):
            term = cw_ref[k:k + 1, :] * cxe_ref[:, first_tap + k:first_tap + k + l, :]
            conv = term if conv is None else conv + term
        conv = conv.reshape(tm, tn)
        st_ref[...] = cxe_ref[:, l + first_tap:l + halo, :]
    else:
        @pl.when(i % tiles_per_seq == 0)
        def _():
            carry_ref[n] = jnp.zeros((halo, tn), F32)

        cxe_ref[0:halo, :] = carry_ref[n]
        cxe_ref[halo:, :] = cx
        carry_ref[n] = cxe_ref[tm:tm + halo, :]
        conv = None
        for k in range(SCONV_WIDTH):
            term = cw_ref[k:k + 1, :] * cxe_ref[first_tap + k:first_tap + k + tm, :]
            conv = term if conv is None else conv + term
        st_ref[i // tiles_per_seq, n] = cxe_ref[tm + first_tap:tm + halo, :]

    gated = (b_gate * conv).astype(BF16)
    o_ref[...] += _dot(gated, wo_ref[...])


def _sconv(x, g, w_in, conv_w, w_out, hist8, n_seq):
    m, d = x.shape
    sample = hist8 is not None
    tn = COL_TILE
    nb = d // tn
    hist_rows = SCONV_WIDTH - 1
    if sample:
        tm = m
        tiles_per_seq = None
        l = m // n_seq
    else:
        tm = TOKEN_TILE
        tiles_per_seq = (m // n_seq) // tm
    in_specs = [
        pl.BlockSpec((tm, d), lambda i, n: (i, 0)),
        pl.BlockSpec((1, d), lambda i, n: (0, 0)),
        pl.BlockSpec((d, tn), lambda i, n: (0, n)),
        pl.BlockSpec((d, tn), lambda i, n: (0, nb + n)),
        pl.BlockSpec((d, tn), lambda i, n: (0, 2 * nb + n)),
        pl.BlockSpec((SCONV_WIDTH, tn), lambda i, n: (0, n)),
        pl.BlockSpec((tn, d), lambda i, n: (n, 0)),
    ]
    args = [x, g, w_in, w_in, w_in, conv_w, w_out]
    if sample:
        in_specs.append(pl.BlockSpec((n_seq, SCONV_HALO, tn), lambda i, n: (0, 0, n)))
        args.append(hist8)
        st_spec = pl.BlockSpec((n_seq, hist_rows, tn), lambda i, n: (0, 0, n))
        scratch = [pltpu.VMEM((tm, d), BF16), pltpu.VMEM((n_seq, l + SCONV_HALO, tn), F32)]
    else:
        st_spec = pl.BlockSpec((n_seq, nb, hist_rows, tn), lambda i, n: (0, 0, 0, 0))
        scratch = [pltpu.VMEM((tm, d), BF16), pltpu.VMEM((tm + SCONV_HALO, tn), F32),
                   pltpu.VMEM((nb, SCONV_HALO, tn), F32)]
    st_shape = (n_seq, hist_rows, d) if sample else (n_seq, nb, hist_rows, tn)
    out, state = pl.pallas_call(
        functools.partial(_sconv_kernel, sample=sample, tiles_per_seq=tiles_per_seq),
        grid=(m // tm, nb),
        in_specs=in_specs,
        out_specs=[pl.BlockSpec((tm, d), lambda i, n: (i, 0)), st_spec],
        out_shape=[jax.ShapeDtypeStruct((m, d), F32),
                   jax.ShapeDtypeStruct(st_shape, F32)],
        scratch_shapes=scratch,
        compiler_params=_params(2),
        name="sconv_sample" if sample else "sconv_prompt",
    )(*args)
    return out, _merge_col_blocks(state)


def _cconv_kernel(x_ref, xc_ref, g_ref, wa_ref, wg_ref, ba_ref, bg_ref, dw_ref, dwb_ref,
                  lng_ref, lnb_ref, w2_ref, b2_ref, *rest, sample, tiles_per_seq, nb):
    if sample:
        hist_ref, o_ref, st_ref, h_ref, glue_ref, conv_ref, z_ref = rest
    else:
        o_ref, st_ref, h_ref, glue_ref, conv_ref, z_ref, carry_ref = rest
    i = pl.program_id(0)
    j = pl.program_id(1)
    tm = x_ref.shape[0]
    halo = CCONV_HALO
    first_tap = halo - (CCONV_WIDTH - 1)

    @pl.when(j == 0)
    def _():
        h_ref[...] = _rms(x_ref[...], g_ref[...]).astype(BF16)

    @pl.when(j < nb)
    def _():
        h = h_ref[...]
        a = _dot(h, wa_ref[...]) + ba_ref[...]
        gt = _dot(h, wg_ref[...]) + bg_ref[...]
        glu = a * jax.nn.sigmoid(gt)
        tn = glu.shape[1]
        if sample:
            bt, l = glue_ref.shape[0], glue_ref.shape[1] - halo
            glue_ref[:, 0:halo, :] = hist_ref[...]
            glue_ref[:, halo:, :] = glu.reshape(bt, l, tn)
            conv = None
            for k in range(CCONV_WIDTH):
                term = dw_ref[k:k + 1, :] * glue_ref[:, first_tap + k:first_tap + k + l, :]
                conv = term if conv is None else conv + term
            conv = conv.reshape(tm, tn)
            st_ref[...] = glue_ref[:, l + first_tap:l + halo, :]
        else:
            @pl.when(i % tiles_per_seq == 0)
            def _():
                carry_ref[j] = jnp.zeros((halo, tn), F32)

            glue_ref[0:halo, :] = carry_ref[j]
            glue_ref[halo:, :] = glu
            carry_ref[j] = glue_ref[tm:tm + halo, :]
            conv = None
            for k in range(CCONV_WIDTH):
                term = dw_ref[k:k + 1, :] * glue_ref[first_tap + k:first_tap + k + tm, :]
                conv = term if conv is None else conv + term
            st_ref[i // tiles_per_seq, j] = glue_ref[tm + first_tap:tm + halo, :]
        conv_ref[j] = conv + dwb_ref[...]

    @pl.when(j == nb)
    def _():
        c = jnp.concatenate([conv_ref[q] for q in range(nb)], axis=1)
        y = _layernorm(c, lng_ref[...], lnb_ref[...])
        z_ref[...] = (y * jax.nn.sigmoid(y)).astype(BF16)

    @pl.when(j >= nb)
    def _():
        o_ref[...] = xc_ref[...] + _dot(z_ref[...], w2_ref[...]) + b2_ref[...]


def _cconv(x, g, w1, b1, dw_w, dw_b, ln_g, ln_b, w2, b2, hist32, n_seq):
    m, d = x.shape
    sample = hist32 is not None
    tn = COL_TILE
    nb = d // tn
    hist_rows = CCONV_WIDTH - 1
    if sample:
        tm = m
        tiles_per_seq = None
        l = m // n_seq
    else:
        tm = TOKEN_TILE
        tiles_per_seq = (m // n_seq) // tm

    def lo(j):
        return jnp.minimum(j, nb - 1)

    def hi(j):
        return jnp.maximum(j - nb, 0)

    in_specs = [
        pl.BlockSpec((tm, d), lambda i, j: (i, 0)),
        pl.BlockSpec((tm, tn), lambda i, j: (i, hi(j))),
        pl.BlockSpec((1, d), lambda i, j: (0, 0)),
        pl.BlockSpec((d, tn), lambda i, j: (0, lo(j))),
        pl.BlockSpec((d, tn), lambda i, j: (0, nb + lo(j))),
        pl.BlockSpec((1, tn), lambda i, j: (0, lo(j))),
        pl.BlockSpec((1, tn), lambda i, j: (0, nb + lo(j))),
        pl.BlockSpec((CCONV_WIDTH, tn), lambda i, j: (0, lo(j))),
        pl.BlockSpec((1, tn), lambda i, j: (0, lo(j))),
        pl.BlockSpec((1, d), lambda i, j: (0, 0)),
        pl.BlockSpec((1, d), lambda i, j: (0, 0)),
        pl.BlockSpec((d, tn), lambda i, j: (0, hi(j))),
        pl.BlockSpec((1, tn), lambda i, j: (0, hi(j))),
    ]
    args = [x, x, g, w1, w1, b1, b1, dw_w, dw_b, ln_g, ln_b, w2, b2]
    common_scratch = [pltpu.VMEM((nb, tm, tn), F32), pltpu.VMEM((tm, d), BF16)]
    if sample:
        in_specs.append(pl.BlockSpec((n_seq, CCONV_HALO, tn), lambda i, j: (0, 0, lo(j))))
        args.append(hist32)
        st_spec = pl.BlockSpec((n_seq, hist_rows, tn), lambda i, j: (0, 0, lo(j)))
        scratch = [pltpu.VMEM((tm, d), BF16), pltpu.VMEM((n_seq, l + CCONV_HALO, tn), F32)] + common_scratch
    else:
        st_spec = pl.BlockSpec((n_seq, nb, hist_rows, tn), lambda i, j: (0, 0, 0, 0))
        scratch = ([pltpu.VMEM((tm, d), BF16), pltpu.VMEM((tm + CCONV_HALO, tn), F32)] + common_scratch
                   + [pltpu.VMEM((nb, CCONV_HALO, tn), F32)])
    st_shape = (n_seq, hist_rows, d) if sample else (n_seq, nb, hist_rows, tn)
    out, state = pl.pallas_call(
        functools.partial(_cconv_kernel, sample=sample, tiles_per_seq=tiles_per_seq, nb=nb),
        grid=(m // tm, 2 * nb),
        in_specs=in_specs,
        out_specs=[pl.BlockSpec((tm, tn), lambda i, j: (i, hi(j))), st_spec],
        out_shape=[jax.ShapeDtypeStruct((m, d), F32),
                   jax.ShapeDtypeStruct(st_shape, F32)],
        scratch_shapes=scratch,
        compiler_params=_params(2),
        name="cconv_sample" if sample else "cconv_prompt",
    )(*args)
    return out, _merge_col_blocks(state)


def _pad_front(hist, rows):
    return jnp.pad(hist, ((0, 0), (rows - hist.shape[1], 0), (0, 0)))


def _trunk(x, pool_hist, sconv_hist, cconv_hist, pos0, p):
    b, s, d = x.shape
    sample = pool_hist is not None
    row = lambda v: v.reshape(1, -1)

    if sample:
        x, pool_state = _pool_sample(x, _pad_front(pool_hist, POOL_HALO), row(p['norm_mix_g'][0]),
                                     p['pool_w'], row(p['pool_scale']), pos0)
    else:
        x, pool_state = _pool_prompt(x, row(p['norm_mix_g'][0]), p['pool_w'], row(p['pool_scale']), pos0)
    x = x.reshape(b * s, d)
    x = _ffn(x, row(p['norm_ffn_g'][0]), p['ffn_w_gate'][0], p['ffn_w_up'][0], p['ffn_w_down'][0])

    u, v = _gmlp_in(x, row(p['norm_mix_g'][1]), p['gmlp_w_in'], row(p['gmlp_b_in']),
                    row(p['gmlp_ln_g']), row(p['gmlp_ln_b']))
    if sample:
        reps = GMLP_CHUNK // s
        ws = jnp.tile(p['gmlp_w_s'][:, :s, :s], (1, reps, reps))
        bs = jnp.tile(p['gmlp_b_s'][:, :s], (1, reps))
        block_diag = s
    else:
        ws, bs, block_diag = p['gmlp_w_s'], p['gmlp_b_s'], 0
    x = _gmlp_out(x, u, v, ws, bs[:, :, None], p['gmlp_w_out'], block_diag)
    x = _ffn(x, row(p['norm_ffn_g'][1]), p['ffn_w_gate'][1], p['ffn_w_up'][1], p['ffn_w_down'][1])

    hist8 = _pad_front(sconv_hist, SCONV_HALO) if sample else None
    x, sconv_state = _sconv(x, row(p['norm_mix_g'][2]), p['sconv_w_in'], p['sconv_conv_w'],
                            p['sconv_w_out'], hist8, b)
    x = _ffn(x, row(p['norm_ffn_g'][2]), p['ffn_w_gate'][2], p['ffn_w_up'][2], p['ffn_w_down'][2])

    hist32 = _pad_front(cconv_hist, CCONV_HALO) if sample else None
    x, cconv_state = _cconv(x, row(p['norm_mix_g'][3]), p['cconv_w_pw1'], row(p['cconv_b_pw1']),
                            p['cconv_dw_w'], row(p['cconv_dw_b']), row(p['cconv_ln_g']),
                            row(p['cconv_ln_b']), p['cconv_w_pw2'], row(p['cconv_b_pw2']), hist32, b)
    y = _ffn(x, row(p['norm_ffn_g'][3]), p['ffn_w_gate'][3], p['ffn_w_up'][3], p['ffn_w_down'][3],
             g_final=row(p['norm_final_g']))
    return y.reshape(b, s, d), pool_state, v.reshape(b, s, -1), sconv_state, cconv_state


MATMUL_WEIGHTS = ('pool_w', 'gmlp_w_in', 'gmlp_w_out', 'sconv_w_in', 'sconv_w_out',
                  'cconv_w_pw1', 'cconv_w_pw2', 'ffn_w_gate', 'ffn_w_up', 'ffn_w_down')


def kernel(x_prompt, x_sample, state_pool, state_sconv, state_cconv, norm_mix_g, norm_ffn_g, norm_final_g, pool_w, pool_scale, gmlp_w_in, gmlp_b_in, gmlp_ln_g, gmlp_ln_b, gmlp_w_s, gmlp_b_s, gmlp_w_out, sconv_w_in, sconv_conv_w, sconv_w_out, cconv_w_pw1, cconv_b_pw1, cconv_dw_w, cconv_dw_b, cconv_ln_g, cconv_ln_b, cconv_w_pw2, cconv_b_pw2, ffn_w_gate, ffn_w_up, ffn_w_down):
    p = dict(norm_mix_g=norm_mix_g, norm_ffn_g=norm_ffn_g, norm_final_g=norm_final_g,
             pool_w=pool_w, pool_scale=pool_scale,
             gmlp_w_in=gmlp_w_in, gmlp_b_in=gmlp_b_in, gmlp_ln_g=gmlp_ln_g, gmlp_ln_b=gmlp_ln_b,
             gmlp_w_s=gmlp_w_s, gmlp_b_s=gmlp_b_s, gmlp_w_out=gmlp_w_out,
             sconv_w_in=sconv_w_in, sconv_conv_w=sconv_conv_w, sconv_w_out=sconv_w_out,
             cconv_w_pw1=cconv_w_pw1, cconv_b_pw1=cconv_b_pw1, cconv_dw_w=cconv_dw_w, cconv_dw_b=cconv_dw_b,
             cconv_ln_g=cconv_ln_g, cconv_ln_b=cconv_ln_b, cconv_w_pw2=cconv_w_pw2, cconv_b_pw2=cconv_b_pw2,
             ffn_w_gate=ffn_w_gate, ffn_w_up=ffn_w_up, ffn_w_down=ffn_w_down)
    for name in MATMUL_WEIGHTS:
        p[name] = p[name].astype(BF16)
    y_p, pool_p, _, sconv_p, cconv_p = _trunk(x_prompt, None, None, None, 0, p)
    y_s, pool_s, gmlp_v_s, sconv_s, cconv_s = _trunk(x_sample, state_pool, state_sconv, state_cconv,
                                                     PAST_LEN, p)
    return (y_p, y_s, pool_p, pool_s, gmlp_v_s, sconv_p, sconv_s, cconv_p, cconv_s)
```

```python
import functools

import jax
import jax.numpy as jnp
from jax import lax
from jax.experimental import pallas as pl
from jax.experimental.pallas import tpu as pltpu

F32 = jnp.float32
BF16 = jnp.bfloat16

EPS = 1e-6
PAST_LEN = 4096
POOL_WINDOWS = (2, 4, 8, 16)
POOL_HIST = max(POOL_WINDOWS) - 1
POOL_HALO = 16
GMLP_CHUNK = 128
GMLP_CAUSAL_SHIFT = 6
GMLP_HEADS = 8
SCONV_WIDTH = 3
SCONV_HALO = 8
CCONV_WIDTH = 31
CCONV_HALO = 32

SUBLANES = 8
TOKEN_TILE = 512
CCONV_TOKEN_TILE = 256
COL_TILE = 512
VMEM_LIMIT_BYTES = 56 * 1024 * 1024


def _rms(x, g):
    ms = jnp.mean(x * x, axis=-1, keepdims=True)
    return x * lax.rsqrt(ms + EPS) * g


def _layernorm(x, g, b):
    mu = jnp.mean(x, axis=-1, keepdims=True)
    xc = x - mu
    var = jnp.mean(xc * xc, axis=-1, keepdims=True)
    return xc * lax.rsqrt(var + EPS) * g + b


def _dot(a, b):
    return jnp.dot(a, b, preferred_element_type=F32)


def _params(n_axes):
    return pltpu.CompilerParams(
        dimension_semantics=("arbitrary",) * n_axes,
        vmem_limit_bytes=VMEM_LIMIT_BYTES)


def _ffn_kernel(x_ref, g_ref, wg_ref, wu_ref, wd_ref, *rest, final_norm):
    if final_norm:
        gf_ref, o_ref, h_ref = rest
    else:
        o_ref, h_ref = rest
    k = pl.program_id(1)

    @pl.when(k == 0)
    def _():
        x = x_ref[...]
        h_ref[...] = _rms(x, g_ref[...]).astype(BF16)
        o_ref[...] = x

    h = h_ref[...]
    gate = _dot(h, wg_ref[...])
    up = _dot(h, wu_ref[...])
    act = (gate * jax.nn.sigmoid(gate) * up).astype(BF16)
    o_ref[...] += _dot(act, wd_ref[...])

    if final_norm:
        @pl.when(k == pl.num_programs(1) - 1)
        def _():
            o_ref[...] = _rms(o_ref[...], gf_ref[...])


def _ffn(x, g, wg, wu, wd, layer, g_final=None):
    m, d = x.shape
    f = wg.shape[2]
    tm = min(TOKEN_TILE, m)
    tf = COL_TILE
    final_norm = g_final is not None
    in_specs = [
        pl.BlockSpec((tm, d), lambda i, k: (i, 0)),
        pl.BlockSpec((1, d), lambda i, k: (0, 0)),
        pl.BlockSpec((None, d, tf), lambda i, k: (layer, 0, k)),
        pl.BlockSpec((None, d, tf), lambda i, k: (layer, 0, k)),
        pl.BlockSpec((None, tf, d), lambda i, k: (layer, k, 0)),
    ]
    args = [x, g, wg, wu, wd]
    if final_norm:
        in_specs.append(pl.BlockSpec((1, d), lambda i, k: (0, 0)))
        args.append(g_final)
    return pl.pallas_call(
        functools.partial(_ffn_kernel, final_norm=final_norm),
        grid=(m // tm, f // tf),
        in_specs=in_specs,
        out_specs=pl.BlockSpec((tm, d), lambda i, k: (i, 0)),
        out_shape=jax.ShapeDtypeStruct((m, d), F32),
        scratch_shapes=[pltpu.VMEM((tm, d), BF16)],
        compiler_params=_params(2),
        name="ffn_final" if final_norm else "ffn",
    )(*args)


def _pool_body(x_ref, hist, g_ref, pw_ref, sc_ref, o_ref, xe_ref, pos_first):
    bt, l, d = x_ref.shape
    grp = d // len(POOL_WINDOWS)
    x = x_ref[...]
    xe_ref[:, 0:POOL_HALO, :] = hist
    xe_ref[:, POOL_HALO:, :] = _rms(x, g_ref[...])
    pos = lax.broadcasted_iota(jnp.int32, (1, l, 1), 1) + pos_first
    for gi, w in enumerate(POOL_WINDOWS):
        c0 = gi * grp
        hg = xe_ref[:, POOL_HALO:, c0:c0 + grp]
        acc = hg
        for k in range(1, w):
            acc = acc + xe_ref[:, POOL_HALO - k:POOL_HALO - k + l, c0:c0 + grp]
        inv_cnt = 1.0 / jnp.minimum(pos + 1, w).astype(F32)
        diff = (acc * inv_cnt - hg).reshape(bt * l, grp).astype(BF16)
        y = _dot(diff, pw_ref[gi]) * sc_ref[:, c0:c0 + grp]
        o_ref[:, :, c0:c0 + grp] = x_ref[:, :, c0:c0 + grp] + y.reshape(bt, l, grp)


def _pool_prompt_kernel(x_ref, g_ref, pw_ref, sc_ref, o_ref, st_ref, xe_ref, carry_ref, *, pos0):
    i = pl.program_id(1)
    l = x_ref.shape[1]

    @pl.when(i == 0)
    def _():
        carry_ref[...] = jnp.zeros_like(carry_ref)

    _pool_body(x_ref, carry_ref[...], g_ref, pw_ref, sc_ref, o_ref, xe_ref, pos0 + i * l)
    carry_ref[...] = xe_ref[:, l:l + POOL_HALO, :]
    st_ref[...] = xe_ref[:, l + POOL_HALO - POOL_HIST:l + POOL_HALO, :]


def _pool_sample_kernel(x_ref, hist_ref, g_ref, pw_ref, sc_ref, o_ref, st_ref, xe_ref, *, pos0):
    l = x_ref.shape[1]
    _pool_body(x_ref, hist_ref[...], g_ref, pw_ref, sc_ref, o_ref, xe_ref, pos0)
    st_ref[...] = xe_ref[:, l + POOL_HALO - POOL_HIST:l + POOL_HALO, :]


def _pool_prompt(x, g, pw, scale, pos0):
    b, s, d = x.shape
    ts = TOKEN_TILE
    ng = len(POOL_WINDOWS)
    return pl.pallas_call(
        functools.partial(_pool_prompt_kernel, pos0=pos0),
        grid=(b, s // ts),
        in_specs=[
            pl.BlockSpec((1, ts, d), lambda bi, i: (bi, i, 0)),
            pl.BlockSpec((1, d), lambda bi, i: (0, 0)),
            pl.BlockSpec((ng, d // ng, d // ng), lambda bi, i: (0, 0, 0)),
            pl.BlockSpec((1, d), lambda bi, i: (0, 0)),
        ],
        out_specs=[
            pl.BlockSpec((1, ts, d), lambda bi, i: (bi, i, 0)),
            pl.BlockSpec((1, POOL_HIST, d), lambda bi, i: (bi, 0, 0)),
        ],
        out_shape=[jax.ShapeDtypeStruct((b, s, d), F32),
                   jax.ShapeDtypeStruct((b, POOL_HIST, d), F32)],
        scratch_shapes=[pltpu.VMEM((1, ts + POOL_HALO, d), F32),
                        pltpu.VMEM((1, POOL_HALO, d), F32)],
        compiler_params=_params(2),
        name="pool_prompt",
    )(x, g, pw, scale)


def _pool_sample(x, hist16, g, pw, scale, pos0):
    b, s, d = x.shape
    bt = 8
    ng = len(POOL_WINDOWS)
    return pl.pallas_call(
        functools.partial(_pool_sample_kernel, pos0=pos0),
        grid=(b // bt,),
        in_specs=[
            pl.BlockSpec((bt, s, d), lambda i: (i, 0, 0)),
            pl.BlockSpec((bt, POOL_HALO, d), lambda i: (i, 0, 0)),
            pl.BlockSpec((1, d), lambda i: (0, 0)),
            pl.BlockSpec((ng, d // ng, d // ng), lambda i: (0, 0, 0)),
            pl.BlockSpec((1, d), lambda i: (0, 0)),
        ],
        out_specs=[
            pl.BlockSpec((bt, s, d), lambda i: (i, 0, 0)),
            pl.BlockSpec((bt, POOL_HIST, d), lambda i: (i, 0, 0)),
        ],
        out_shape=[jax.ShapeDtypeStruct((b, s, d), F32),
                   jax.ShapeDtypeStruct((b, POOL_HIST, d), F32)],
        scratch_shapes=[pltpu.VMEM((bt, s + POOL_HALO, d), F32)],
        compiler_params=_params(1),
        name="pool_sample",
    )(x, hist16, g, pw, scale)


def _gmlp_in_kernel(x_ref, g_ref, w_ref, b_ref, lng_ref, lnb_ref, u_ref, v_ref, h_ref, vraw_ref):
    j = pl.program_id(1)
    nj = pl.num_programs(1)
    half = nj // 2

    @pl.when(j == 0)
    def _():
        h_ref[...] = _rms(x_ref[...], g_ref[...]).astype(BF16)

    z = jax.nn.gelu(_dot(h_ref[...], w_ref[...]) + b_ref[...])

    @pl.when(j < half)
    def _():
        u_ref[...] = z

    @pl.when(j >= half)
    def _():
        vraw_ref[j - half] = z

    @pl.when(j == nj - 1)
    def _():
        parts = [vraw_ref[c] for c in range(half)]
        v = _layernorm(jnp.concatenate(parts, axis=1), lng_ref[...], lnb_ref[...])
        v_ref[...] = v


def _gmlp_in(x, g, w_in, b_in, ln_g, ln_b):
    m, d = x.shape
    n = w_in.shape[1]
    width = n // 2
    tm = min(TOKEN_TILE, m)
    tn = 1024
    nj = n // tn
    half = nj // 2
    return pl.pallas_call(
        _gmlp_in_kernel,
        grid=(m // tm, nj),
        in_specs=[
            pl.BlockSpec((tm, d), lambda i, j: (i, 0)),
            pl.BlockSpec((1, d), lambda i, j: (0, 0)),
            pl.BlockSpec((d, tn), lambda i, j: (0, j)),
            pl.BlockSpec((1, tn), lambda i, j: (0, j)),
            pl.BlockSpec((1, width), lambda i, j: (0, 0)),
            pl.BlockSpec((1, width), lambda i, j: (0, 0)),
        ],
        out_specs=[
            pl.BlockSpec((tm, tn), lambda i, j: (i, jnp.minimum(j, half - 1))),
            pl.BlockSpec((tm, width), lambda i, j: (i, 0)),
        ],
        out_shape=[jax.ShapeDtypeStruct((m, width), F32),
                   jax.ShapeDtypeStruct((m, width), F32)],
        scratch_shapes=[pltpu.VMEM((tm, d), BF16),
                        pltpu.VMEM((half, tm, tn), F32)],
        compiler_params=_params(2),
        name="gmlp_in",
    )(x, g, w_in, b_in, ln_g, ln_b)


def _gmlp_out_kernel(x_ref, u_ref, v_ref, ws_ref, bs_ref, wo_ref, o_ref, *, block_diag):
    hd = pl.program_id(1)

    @pl.when(hd == 0)
    def _():
        o_ref[...] = x_ref[...]

    row = lax.broadcasted_iota(jnp.int32, (GMLP_CHUNK, GMLP_CHUNK), 0)
    col = lax.broadcasted_iota(jnp.int32, (GMLP_CHUNK, GMLP_CHUNK), 1)
    if block_diag:
        shift = block_diag.bit_length() - 1
        mask = (row >> shift) == (col >> shift)
    else:
        mask = (col >> GMLP_CAUSAL_SHIFT) <= (row >> GMLP_CAUSAL_SHIFT)
    wm = jnp.where(mask, ws_ref[0], 0.0).astype(BF16)
    bias = bs_ref[0]
    parts = []
    for c in range(u_ref.shape[0] // GMLP_CHUNK):
        rows = slice(c * GMLP_CHUNK, (c + 1) * GMLP_CHUNK)
        s = _dot(wm, v_ref[rows, :].astype(BF16)) + bias
        parts.append((u_ref[rows, :] * s).astype(BF16))
    t = jnp.concatenate(parts, axis=0)
    o_ref[...] += _dot(t, wo_ref[...])


def _gmlp_out(x, u, v, ws, bs, w_out, block_diag):
    m, d = x.shape
    width = u.shape[1]
    hdim = width // GMLP_HEADS
    tm = min(TOKEN_TILE, m)
    return pl.pallas_call(
        functools.partial(_gmlp_out_kernel, block_diag=block_diag),
        grid=(m // tm, GMLP_HEADS),
        in_specs=[
            pl.BlockSpec((tm, d), lambda i, hd: (i, 0)),
            pl.BlockSpec((tm, hdim), lambda i, hd: (i, hd)),
            pl.BlockSpec((tm, hdim), lambda i, hd: (i, hd)),
            pl.BlockSpec((1, GMLP_CHUNK, GMLP_CHUNK), lambda i, hd: (hd, 0, 0)),
            pl.BlockSpec((1, GMLP_CHUNK, 1), lambda i, hd: (hd, 0, 0)),
            pl.BlockSpec((hdim, d), lambda i, hd: (hd, 0)),
        ],
        out_specs=pl.BlockSpec((tm, d), lambda i, hd: (i, 0)),
        out_shape=jax.ShapeDtypeStruct((m, d), F32),
        compiler_params=_params(2),
        name="gmlp_out",
    )(x, u, v, ws, bs, w_out)


def _merge_col_blocks(state):
    if state.ndim == 3:
        return state
    n_seq, nb, rows, tn = state.shape
    return state.transpose(0, 2, 1, 3).reshape(n_seq, rows, nb * tn)


def _sconv_kernel(x_ref, g_ref, wb_ref, wc_ref, wx_ref, cw_ref, wo_ref, *rest, sample, tiles_per_seq):
    if sample:
        hist_ref, o_ref, st_ref, h_ref, cxe_ref = rest
    else:
        o_ref, st_ref, h_ref, cxe_ref, carry_ref = rest
    i = pl.program_id(0)
    n = pl.program_id(1)
    tm = x_ref.shape[0]
    halo = SCONV_HALO
    first_tap = halo - (SCONV_WIDTH - 1)

    @pl.when(n == 0)
    def _():
        x = x_ref[...]
        h_ref[...] = _rms(x, g_ref[...]).astype(BF16)
        o_ref[...] = x

    h = h_ref[...]
    b_gate = _dot(h, wb_ref[...])
    cx = _dot(h, wc_ref[...]) * _dot(h, wx_ref[...])
    tn = cx.shape[1]

    if sample:
        bt, l = cxe_ref.shape[0], cxe_ref.shape[1] - halo
        cxe_ref[:, 0:halo, :] = hist_ref[...]
        cxe_ref[:, halo:, :] = cx.reshape(bt, l, tn)
        conv = None
        for k in range(SCONV_WIDTH):
            term = cw_ref[k:k + 1, :] * cxe_ref[:, first_tap + k:first_tap + k + l, :]
            conv = term if conv is None else conv + term
        conv = conv.reshape(tm, tn)
        st_ref[...] = cxe_ref[:, l + first_tap:l + halo, :]
    else:
        @pl.when(i % tiles_per_seq == 0)
        def _():
            carry_ref[n] = jnp.zeros((halo, tn), F32)

        cxe_ref[0:halo, :] = carry_ref[n]
        cxe_ref[halo:, :] = cx
        carry_ref[n] = cxe_ref[tm:tm + halo, :]
        conv = None
        for k in range(SCONV_WIDTH):
            term = cw_ref[k:k + 1, :] * cxe_ref[first_tap + k:first_tap + k + tm, :]
            conv = term if conv is None else conv + term
        st_ref[i // tiles_per_seq, n] = cxe_ref[tm + first_tap:tm + halo, :]

    gated = (b_gate * conv).astype(BF16)
    o_ref[...] += _dot(gated, wo_ref[...])


def _sconv(x, g, w_in, conv_w, w_out, hist8, n_seq):
    m, d = x.shape
    sample = hist8 is not None
    tn = COL_TILE
    nb = d // tn
    hist_rows = SCONV_WIDTH - 1
    if sample:
        tm = m
        tiles_per_seq = None
        l = m // n_seq
    else:
        tm = TOKEN_TILE
        tiles_per_seq = (m // n_seq) // tm
    in_specs = [
        pl.BlockSpec((tm, d), lambda i, n: (i, 0)),
        pl.BlockSpec((1, d), lambda i, n: (0, 0)),
        pl.BlockSpec((d, tn), lambda i, n: (0, n)),
        pl.BlockSpec((d, tn), lambda i, n: (0, nb + n)),
        pl.BlockSpec((d, tn), lambda i, n: (0, 2 * nb + n)),
        pl.BlockSpec((SCONV_WIDTH, tn), lambda i, n: (0, n)),
        pl.BlockSpec((tn, d), lambda i, n: (n, 0)),
    ]
    args = [x, g, w_in, w_in, w_in, conv_w, w_out]
    if sample:
        in_specs.append(pl.BlockSpec((n_seq, SCONV_HALO, tn), lambda i, n: (0, 0, n)))
        args.append(hist8)
        st_spec = pl.BlockSpec((n_seq, hist_rows, tn), lambda i, n: (0, 0, n))
        scratch = [pltpu.VMEM((tm, d), BF16), pltpu.VMEM((n_seq, l + SCONV_HALO, tn), F32)]
    else:
        st_spec = pl.BlockSpec((n_seq, nb, hist_rows, tn), lambda i, n: (0, 0, 0, 0))
        scratch = [pltpu.VMEM((tm, d), BF16), pltpu.VMEM((tm + SCONV_HALO, tn), F32),
                   pltpu.VMEM((nb, SCONV_HALO, tn), F32)]
    st_shape = (n_seq, hist_rows, d) if sample else (n_seq, nb, hist_rows, tn)
    out, state = pl.pallas_call(
        functools.partial(_sconv_kernel, sample=sample, tiles_per_seq=tiles_per_seq),
        grid=(m // tm, nb),
        in_specs=in_specs,
        out_specs=[pl.BlockSpec((tm, d), lambda i, n: (i, 0)), st_spec],
        out_shape=[jax.ShapeDtypeStruct((m, d), F32),
                   jax.ShapeDtypeStruct(st_shape, F32)],
        scratch_shapes=scratch,
        compiler_params=_params(2),
        name="sconv_sample" if sample else "sconv_prompt",
    )(*args)
    return out, _merge_col_blocks(state)


def _causal_taps_flat(glue_ref, n, w_ref, c0, tn, tm, halo, width):
    first_tap = halo - (width - 1)
    ext = tm + SUBLANES
    conv = None
    for r in range(SUBLANES):
        part = None
        for o in range(first_tap, first_tap + width):
            if o % SUBLANES != r:
                continue
            k = o - first_tap
            rows = tm if r == 0 else ext
            term = w_ref[k:k + 1, c0:c0 + tn] * glue_ref[n, o - r:o - r + rows, :]
            part = term if part is None else part + term
        if part is None:
            continue
        if r:
            part = pltpu.roll(part, ext - r, axis=0)[:tm]
        conv = part if conv is None else conv + part
    return conv


def _cconv_kernel(x_ref, g_ref, w1_ref, b1_ref, dw_ref, dwb_ref, lng_ref, lnb_ref, w2_ref, b2_ref,
                  *rest, sample, tiles_per_seq, nb):
    if sample:
        hist_ref, o_ref, st_ref, glue_ref, conv_ref = rest
    else:
        o_ref, st_ref, glue_ref, conv_ref, carry_ref = rest
    i = pl.program_id(0)
    tm, d = x_ref.shape
    tn = d // nb
    halo = CCONV_HALO
    first_tap = halo - (CCONV_WIDTH - 1)

    x = x_ref[...]
    h = _rms(x, g_ref[...]).astype(BF16)

    if not sample:
        @pl.when(i % tiles_per_seq == 0)
        def _():
            carry_ref[...] = jnp.zeros_like(carry_ref)

    for n in range(nb):
        c0 = n * tn
        a = _dot(h, w1_ref[:, c0:c0 + tn]) + b1_ref[:, c0:c0 + tn]
        gt = _dot(h, w1_ref[:, d + c0:d + c0 + tn]) + b1_ref[:, d + c0:d + c0 + tn]
        glu = a * jax.nn.sigmoid(gt)
        if sample:
            bt, l = glue_ref.shape[0], glue_ref.shape[1] - halo
            glue_ref[:, 0:halo, :] = hist_ref[:, :, c0:c0 + tn]
            glue_ref[:, halo:, :] = glu.reshape(bt, l, tn)
            conv = None
            for k in range(CCONV_WIDTH):
                term = dw_ref[k:k + 1, c0:c0 + tn] * glue_ref[:, first_tap + k:first_tap + k + l, :]
                conv = term if conv is None else conv + term
            conv = conv.reshape(tm, tn)
            st_ref[:, :, c0:c0 + tn] = glue_ref[:, l + first_tap:l + halo, :]
        else:
            glue_ref[n, 0:halo, :] = carry_ref[n]
            glue_ref[n, halo:, :] = glu
            carry_ref[n] = glue_ref[n, tm:tm + halo, :]
            conv = _causal_taps_flat(glue_ref, n, dw_ref, c0, tn, tm, halo, CCONV_WIDTH)
            st_ref[i // tiles_per_seq, n] = glue_ref[n, tm + first_tap:tm + halo, :]
        conv_ref[:, c0:c0 + tn] = conv + dwb_ref[:, c0:c0 + tn]

    y = _layernorm(conv_ref[...], lng_ref[...], lnb_ref[...])
    z = (y * jax.nn.sigmoid(y)).astype(BF16)
    o_ref[...] = x + _dot(z, w2_ref[...]) + b2_ref[...]


def _cconv(x, g, w1, b1, dw_w, dw_b, ln_g, ln_b, w2, b2, hist32, n_seq):
    m, d = x.shape
    sample = hist32 is not None
    tn = COL_TILE
    nb = d // tn
    hist_rows = CCONV_WIDTH - 1
    tm = CCONV_TOKEN_TILE
    l = m // n_seq
    tiles_per_seq = None if sample else l // tm
    bt = tm // l if sample else None

    def resident(shape):
        return pl.BlockSpec(shape, lambda i: (0,) * len(shape), pipeline_mode=pl.Buffered(1))

    in_specs = [
        pl.BlockSpec((tm, d), lambda i: (i, 0)),
        resident((1, d)),
        resident((d, 2 * d)),
        resident((1, 2 * d)),
        resident((CCONV_WIDTH, d)),
        resident((1, d)),
        resident((1, d)),
        resident((1, d)),
        resident((d, d)),
        resident((1, d)),
    ]
    args = [x, g, w1, b1, dw_w, dw_b, ln_g, ln_b, w2, b2]
    if sample:
        in_specs.append(pl.BlockSpec((bt, CCONV_HALO, d), lambda i: (i, 0, 0), pipeline_mode=pl.Buffered(1)))
        args.append(hist32)
        st_spec = pl.BlockSpec((bt, hist_rows, d), lambda i: (i, 0, 0))
        scratch = [pltpu.VMEM((bt, l + CCONV_HALO, tn), F32), pltpu.VMEM((tm, d), F32)]
    else:
        st_spec = pl.BlockSpec((n_seq, nb, hist_rows, tn), lambda i: (0, 0, 0, 0))
        scratch = [pltpu.VMEM((nb, tm + CCONV_HALO, tn), F32), pltpu.VMEM((tm, d), F32),
                   pltpu.VMEM((nb, CCONV_HALO, tn), F32)]
    st_shape = (n_seq, hist_rows, d) if sample else (n_seq, nb, hist_rows, tn)
    out, state = pl.pallas_call(
        functools.partial(_cconv_kernel, sample=sample, tiles_per_seq=tiles_per_seq, nb=nb),
        grid=(m // tm,),
        in_specs=in_specs,
        out_specs=[pl.BlockSpec((tm, d), lambda i: (i, 0)), st_spec],
        out_shape=[jax.ShapeDtypeStruct((m, d), F32),
                   jax.ShapeDtypeStruct(st_shape, F32)],
        scratch_shapes=scratch,
        compiler_params=_params(1),
        name="cconv_sample" if sample else "cconv_prompt",
    )(*args)
    return out, _merge_col_blocks(state)


def _pad_front(hist, rows):
    return jnp.pad(hist, ((0, 0), (rows - hist.shape[1], 0), (0, 0)))


def _trunk(x, pool_hist, sconv_hist, cconv_hist, pos0, p):
    b, s, d = x.shape
    sample = pool_hist is not None
    row = lambda v: v.reshape(1, -1)

    if sample:
        x, pool_state = _pool_sample(x, _pad_front(pool_hist, POOL_HALO), row(p['norm_mix_g'][0]),
                                     p['pool_w'], row(p['pool_scale']), pos0)
    else:
        x, pool_state = _pool_prompt(x, row(p['norm_mix_g'][0]), p['pool_w'], row(p['pool_scale']), pos0)
    x = x.reshape(b * s, d)
    ffn = lambda x, layer, g_final=None: _ffn(x, row(p['norm_ffn_g'][layer]), p['ffn_w_gate'], p['ffn_w_up'],
                                              p['ffn_w_down'], layer, g_final)
    x = ffn(x, 0)

    u, v = _gmlp_in(x, row(p['norm_mix_g'][1]), p['gmlp_w_in'], row(p['gmlp_b_in']),
                    row(p['gmlp_ln_g']), row(p['gmlp_ln_b']))
    if sample:
        reps = GMLP_CHUNK // s
        ws = jnp.tile(p['gmlp_w_s'][:, :s, :s], (1, reps, reps))
        bs = jnp.tile(p['gmlp_b_s'][:, :s], (1, reps))
        block_diag = s
    else:
        ws, bs, block_diag = p['gmlp_w_s'], p['gmlp_b_s'], 0
    x = _gmlp_out(x, u, v, ws, bs[:, :, None], p['gmlp_w_out'], block_diag)
    x = ffn(x, 1)

    hist8 = _pad_front(sconv_hist, SCONV_HALO) if sample else None
    x, sconv_state = _sconv(x, row(p['norm_mix_g'][2]), p['sconv_w_in'], p['sconv_conv_w'],
                            p['sconv_w_out'], hist8, b)
    x = ffn(x, 2)

    hist32 = _pad_front(cconv_hist, CCONV_HALO) if sample else None
    x, cconv_state = _cconv(x, row(p['norm_mix_g'][3]), p['cconv_w_pw1'], row(p['cconv_b_pw1']),
                            p['cconv_dw_w'], row(p['cconv_dw_b']), row(p['cconv_ln_g']),
                            row(p['cconv_ln_b']), p['cconv_w_pw2'], row(p['cconv_b_pw2']), hist32, b)
    y = ffn(x, 3, row(p['norm_final_g']))
    return y.reshape(b, s, d), pool_state, v.reshape(b, s, -1), sconv_state, cconv_state


MATMUL_WEIGHTS = ('pool_w', 'gmlp_w_in', 'gmlp_w_out', 'sconv_w_in', 'sconv_w_out',
                  'cconv_w_pw1', 'cconv_w_pw2', 'ffn_w_gate', 'ffn_w_up', 'ffn_w_down')


def kernel(x_prompt, x_sample, state_pool, state_sconv, state_cconv, norm_mix_g, norm_ffn_g, norm_final_g, pool_w, pool_scale, gmlp_w_in, gmlp_b_in, gmlp_ln_g, gmlp_ln_b, gmlp_w_s, gmlp_b_s, gmlp_w_out, sconv_w_in, sconv_conv_w, sconv_w_out, cconv_w_pw1, cconv_b_pw1, cconv_dw_w, cconv_dw_b, cconv_ln_g, cconv_ln_b, cconv_w_pw2, cconv_b_pw2, ffn_w_gate, ffn_w_up, ffn_w_down):
    p = dict(norm_mix_g=norm_mix_g, norm_ffn_g=norm_ffn_g, norm_final_g=norm_final_g,
             pool_w=pool_w, pool_scale=pool_scale,
             gmlp_w_in=gmlp_w_in, gmlp_b_in=gmlp_b_in, gmlp_ln_g=gmlp_ln_g, gmlp_ln_b=gmlp_ln_b,
             gmlp_w_s=gmlp_w_s, gmlp_b_s=gmlp_b_s, gmlp_w_out=gmlp_w_out,
             sconv_w_in=sconv_w_in, sconv_conv_w=sconv_conv_w, sconv_w_out=sconv_w_out,
             cconv_w_pw1=cconv_w_pw1, cconv_b_pw1=cconv_b_pw1, cconv_dw_w=cconv_dw_w, cconv_dw_b=cconv_dw_b,
             cconv_ln_g=cconv_ln_g, cconv_ln_b=cconv_ln_b, cconv_w_pw2=cconv_w_pw2, cconv_b_pw2=cconv_b_pw2,
             ffn_w_gate=ffn_w_gate, ffn_w_up=ffn_w_up, ffn_w_down=ffn_w_down)
    for name in MATMUL_WEIGHTS:
        p[name] = p[name].astype(BF16)
    y_p, pool_p, _, sconv_p, cconv_p = _trunk(x_prompt, None, None, None, 0, p)
    y_s, pool_s, gmlp_v_s, sconv_s, cconv_s = _trunk(x_sample, state_pool, state_sconv, state_cconv,
                                                     PAST_LEN, p)
    return (y_p, y_s, pool_p, pool_s, gmlp_v_s, sconv_p, sconv_s, cconv_p, cconv_s)
```

```python
import functools

import jax
import jax.numpy as jnp
from jax import lax
from jax.experimental import pallas as pl
from jax.experimental.pallas import tpu as pltpu

F32 = jnp.float32
BF16 = jnp.bfloat16

EPS = 1e-6
PAST_LEN = 4096
POOL_WINDOWS = (2, 4, 8, 16)
POOL_HIST = max(POOL_WINDOWS) - 1
POOL_HALO = 16
GMLP_CHUNK = 128
GMLP_CAUSAL_SHIFT = 6
GMLP_HEADS = 8
GMLP_HEADS_PER_STEP = 2
GMLP_IN_COL_TILE = 1024
GMLP_EMIT_TOKEN_TILE = 256
SCONV_WIDTH = 3
SCONV_HALO = 8
CCONV_WIDTH = 31
CCONV_HALO = 32

SUBLANES = 8
TOKEN_TILE = 512
CCONV_TOKEN_TILE = 256
FFN_TOKEN_TILE = 1024
FFN_CAST_TOKEN_TILE = 512
FFN_CAST_COL_TILE = 256
COL_TILE = 512
VMEM_LIMIT_BYTES = 56 * 1024 * 1024


def _rms(x, g):
    ms = jnp.mean(x * x, axis=-1, keepdims=True)
    return x * lax.rsqrt(ms + EPS) * g


def _layernorm(x, g, b):
    mu = jnp.mean(x, axis=-1, keepdims=True)
    xc = x - mu
    var = jnp.mean(xc * xc, axis=-1, keepdims=True)
    return xc * lax.rsqrt(var + EPS) * g + b


def _dot(a, b):
    return jnp.dot(a, b, preferred_element_type=F32)


def _params(n_axes):
    return pltpu.CompilerParams(
        dimension_semantics=("arbitrary",) * n_axes,
        vmem_limit_bytes=VMEM_LIMIT_BYTES)


def _ffn_kernel(x_ref, g_ref, wg_ref, wu_ref, wd_ref, *rest, final_norm, emit_bf16):
    rest = list(rest)
    gf_ref = rest.pop(0) if final_norm else None
    o_ref = rest.pop(0)
    i = pl.program_id(0)
    k = pl.program_id(1)
    if emit_bf16:
        h_ref = rest.pop()
        x_tile = x_ref
    else:
        h_ref, x_tile, sem = rest
        tm = x_tile.shape[0]

        def fetch(tile):
            return pltpu.make_async_copy(x_ref.at[pl.ds(tile * tm, tm)], x_tile, sem)

    @pl.when(k == 0)
    def _():
        if not emit_bf16:
            @pl.when(i == 0)
            def _():
                fetch(0).start()

            fetch(i).wait()
        x = x_tile[...]
        h_ref[...] = _rms(x, g_ref[...]).astype(BF16)
        o_ref[...] = x

    wg, wu, wd = wg_ref[...], wu_ref[...], wd_ref[...]
    if emit_bf16:
        wg, wu, wd = wg.astype(BF16), wu.astype(BF16), wd.astype(BF16)
        for w, w_out_ref in zip((wg, wu, wd), rest):
            w_out_ref[...] = w
    h = h_ref[...]
    gate = _dot(h, wg)
    up = _dot(h, wu)
    act = (gate * jax.nn.sigmoid(gate) * up).astype(BF16)
    o_ref[...] += _dot(act, wd)

    if not emit_bf16:
        @pl.when((k == 1) & (i + 1 < pl.num_programs(0)))
        def _():
            fetch(i + 1).start()

    if final_norm:
        @pl.when(k == pl.num_programs(1) - 1)
        def _():
            o_ref[...] = _rms(o_ref[...], gf_ref[...])


def _ffn(x, g, wg, wu, wd, layer, g_final=None):
    m, d = x.shape
    emit_bf16 = wg.ndim == 3
    f = wg.shape[-1]
    if emit_bf16:
        tm, tf = FFN_CAST_TOKEN_TILE, FFN_CAST_COL_TILE
        assert m == tm, "every weight block must be visited exactly once"
        w_specs = [
            pl.BlockSpec((None, d, tf), lambda i, k: (layer, 0, k)),
            pl.BlockSpec((None, d, tf), lambda i, k: (layer, 0, k)),
            pl.BlockSpec((None, tf, d), lambda i, k: (layer, k, 0)),
        ]
    else:
        tm, tf = FFN_TOKEN_TILE, COL_TILE
        w_specs = [
            pl.BlockSpec((d, tf), lambda i, k: (0, k)),
            pl.BlockSpec((d, tf), lambda i, k: (0, k)),
            pl.BlockSpec((tf, d), lambda i, k: (k, 0)),
        ]
    final_norm = g_final is not None
    scratch = [pltpu.VMEM((tm, d), BF16)]
    if emit_bf16:
        x_spec = pl.BlockSpec((tm, d), lambda i, k: (i, 0))
    else:
        x_spec = pl.BlockSpec(memory_space=pl.ANY)
        scratch += [pltpu.VMEM((tm, d), F32), pltpu.SemaphoreType.DMA(())]
    in_specs = [x_spec, pl.BlockSpec((1, d), lambda i, k: (0, 0))] + w_specs
    args = [x, g, wg, wu, wd]
    if final_norm:
        in_specs.append(pl.BlockSpec((1, d), lambda i, k: (0, 0)))
        args.append(g_final)
    out_specs = [pl.BlockSpec((tm, d), lambda i, k: (i, 0))]
    out_shape = [jax.ShapeDtypeStruct((m, d), F32)]
    if emit_bf16:
        out_specs += [pl.BlockSpec((d, tf), lambda i, k: (0, k)),
                      pl.BlockSpec((d, tf), lambda i, k: (0, k)),
                      pl.BlockSpec((tf, d), lambda i, k: (k, 0))]
        out_shape += [jax.ShapeDtypeStruct((d, f), BF16), jax.ShapeDtypeStruct((d, f), BF16),
                      jax.ShapeDtypeStruct((f, d), BF16)]
    outs = pl.pallas_call(
        functools.partial(_ffn_kernel, final_norm=final_norm, emit_bf16=emit_bf16),
        grid=(m // tm, f // tf),
        in_specs=in_specs,
        out_specs=out_specs,
        out_shape=out_shape,
        scratch_shapes=scratch,
        compiler_params=_params(2),
        name=("ffn_cast" if emit_bf16 else "ffn") + ("_final" if final_norm else ""),
    )(*args)
    return outs[0], tuple(outs[1:])


def _pool_body(x_ref, hist, g_ref, pw_ref, sc_ref, o_ref, xe_ref, pos_first):
    bt, l, d = x_ref.shape
    grp = d // len(POOL_WINDOWS)
    x = x_ref[...]
    xe_ref[:, 0:POOL_HALO, :] = hist
    xe_ref[:, POOL_HALO:, :] = _rms(x, g_ref[...])
    pos = lax.broadcasted_iota(jnp.int32, (1, l, 1), 1) + pos_first
    for gi, w in enumerate(POOL_WINDOWS):
        c0 = gi * grp
        hg = xe_ref[:, POOL_HALO:, c0:c0 + grp]
        acc = hg
        for k in range(1, w):
            acc = acc + xe_ref[:, POOL_HALO - k:POOL_HALO - k + l, c0:c0 + grp]
        inv_cnt = 1.0 / jnp.minimum(pos + 1, w).astype(F32)
        diff = (acc * inv_cnt - hg).reshape(bt * l, grp).astype(BF16)
        y = _dot(diff, pw_ref[gi]) * sc_ref[:, c0:c0 + grp]
        o_ref[:, :, c0:c0 + grp] = x_ref[:, :, c0:c0 + grp] + y.reshape(bt, l, grp)


def _pool_prompt_kernel(x_ref, g_ref, pw_ref, sc_ref, o_ref, st_ref, xe_ref, carry_ref, *, pos0):
    i = pl.program_id(1)
    l = x_ref.shape[1]

    @pl.when(i == 0)
    def _():
        carry_ref[...] = jnp.zeros_like(carry_ref)

    _pool_body(x_ref, carry_ref[...], g_ref, pw_ref, sc_ref, o_ref, xe_ref, pos0 + i * l)
    carry_ref[...] = xe_ref[:, l:l + POOL_HALO, :]
    st_ref[...] = xe_ref[:, l + POOL_HALO - POOL_HIST:l + POOL_HALO, :]


def _pool_sample_kernel(x_ref, hist_ref, g_ref, pw_ref, sc_ref, o_ref, st_ref, xe_ref, *, pos0):
    l = x_ref.shape[1]
    _pool_body(x_ref, hist_ref[...], g_ref, pw_ref, sc_ref, o_ref, xe_ref, pos0)
    st_ref[...] = xe_ref[:, l + POOL_HALO - POOL_HIST:l + POOL_HALO, :]


def _pool_prompt(x, g, pw, scale, pos0):
    b, s, d = x.shape
    ts = TOKEN_TILE
    ng = len(POOL_WINDOWS)
    return pl.pallas_call(
        functools.partial(_pool_prompt_kernel, pos0=pos0),
        grid=(b, s // ts),
        in_specs=[
            pl.BlockSpec((1, ts, d), lambda bi, i: (bi, i, 0)),
            pl.BlockSpec((1, d), lambda bi, i: (0, 0)),
            pl.BlockSpec((ng, d // ng, d // ng), lambda bi, i: (0, 0, 0)),
            pl.BlockSpec((1, d), lambda bi, i: (0, 0)),
        ],
        out_specs=[
            pl.BlockSpec((1, ts, d), lambda bi, i: (bi, i, 0)),
            pl.BlockSpec((1, POOL_HIST, d), lambda bi, i: (bi, 0, 0)),
        ],
        out_shape=[jax.ShapeDtypeStruct((b, s, d), F32),
                   jax.ShapeDtypeStruct((b, POOL_HIST, d), F32)],
        scratch_shapes=[pltpu.VMEM((1, ts + POOL_HALO, d), F32),
                        pltpu.VMEM((1, POOL_HALO, d), F32)],
        compiler_params=_params(2),
        name="pool_prompt",
    )(x, g, pw, scale)


def _pool_sample(x, hist16, g, pw, scale, pos0):
    b, s, d = x.shape
    bt = 8
    ng = len(POOL_WINDOWS)
    return pl.pallas_call(
        functools.partial(_pool_sample_kernel, pos0=pos0),
        grid=(b // bt,),
        in_specs=[
            pl.BlockSpec((bt, s, d), lambda i: (i, 0, 0)),
            pl.BlockSpec((bt, POOL_HALO, d), lambda i: (i, 0, 0)),
            pl.BlockSpec((1, d), lambda i: (0, 0)),
            pl.BlockSpec((ng, d // ng, d // ng), lambda i: (0, 0, 0)),
            pl.BlockSpec((1, d), lambda i: (0, 0)),
        ],
        out_specs=[
            pl.BlockSpec((bt, s, d), lambda i: (i, 0, 0)),
            pl.BlockSpec((bt, POOL_HIST, d), lambda i: (i, 0, 0)),
        ],
        out_shape=[jax.ShapeDtypeStruct((b, s, d), F32),
                   jax.ShapeDtypeStruct((b, POOL_HIST, d), F32)],
        scratch_shapes=[pltpu.VMEM((bt, s + POOL_HALO, d), F32)],
        compiler_params=_params(1),
        name="pool_sample",
    )(x, hist16, g, pw, scale)


def _gmlp_kernel(x_ref, g_ref, win_ref, bin_ref, lng_ref, lnb_ref, ws_ref, bs_ref, wout_ref, o_ref, *rest,
                 block_diag, emit_v, n_in):
    if emit_v:
        vout_ref, h_ref, z_ref, v_ref = rest
    else:
        h_ref, z_ref, v_ref = rest
    j = pl.program_id(1)
    tm = x_ref.shape[0]
    cw = z_ref.shape[2]
    n_slots = z_ref.shape[0]
    hdim = cw // GMLP_HEADS_PER_STEP

    @pl.when(j == 0)
    def _():
        x = x_ref[...]
        h_ref[...] = _rms(x, g_ref[...]).astype(BF16)
        o_ref[...] = x

    @pl.when(j < n_in)
    def _():
        z = jax.nn.gelu(_dot(h_ref[...], win_ref[...]) + bin_ref[...])
        per_step = z.shape[1] // cw
        for q in range(per_step):
            z_ref[j * per_step + q] = z[:, q * cw:(q + 1) * cw]

    @pl.when(j == n_in - 1)
    def _():
        half = n_slots // 2
        v = _layernorm(jnp.concatenate([z_ref[half + q] for q in range(half)], axis=1),
                       lng_ref[...], lnb_ref[...])
        if emit_v:
            vout_ref[...] = v
        for q in range(half):
            v_ref[q] = v[:, q * cw:(q + 1) * cw].astype(BF16)

    @pl.when(j >= n_in)
    def _():
        q = j - n_in
        row = lax.broadcasted_iota(jnp.int32, (GMLP_CHUNK, GMLP_CHUNK), 0)
        col = lax.broadcasted_iota(jnp.int32, (GMLP_CHUNK, GMLP_CHUNK), 1)
        if block_diag:
            shift = block_diag.bit_length() - 1
            mask = (row >> shift) == (col >> shift)
        else:
            mask = (col >> GMLP_CAUSAL_SHIFT) <= (row >> GMLP_CAUSAL_SHIFT)
        cols = []
        for hh in range(GMLP_HEADS_PER_STEP):
            head = q * GMLP_HEADS_PER_STEP + hh
            wm = jnp.where(mask, ws_ref[head], 0.0).astype(BF16)
            bias = bs_ref[head]
            lanes = slice(hh * hdim, (hh + 1) * hdim)
            parts = []
            for c in range(tm // GMLP_CHUNK):
                rows = slice(c * GMLP_CHUNK, (c + 1) * GMLP_CHUNK)
                s = _dot(wm, v_ref[q, rows, lanes]) + bias
                parts.append((z_ref[q, rows, lanes] * s).astype(BF16))
            cols.append(jnp.concatenate(parts, axis=0))
        t = jnp.concatenate(cols, axis=1)
        o_ref[...] += _dot(t, wout_ref[...])


def _gmlp(x, g, w_in, b_in, ln_g, ln_b, ws, bs, w_out, block_diag, emit_v):
    m, d = x.shape
    n = w_in.shape[1]
    width = n // 2
    tm = GMLP_EMIT_TOKEN_TILE if emit_v else TOKEN_TILE
    tn = GMLP_IN_COL_TILE
    n_in = n // tn
    cw = GMLP_HEADS_PER_STEP * (width // GMLP_HEADS)
    n_out = width // cw

    def in_step(j):
        return jnp.minimum(j, n_in - 1)

    def out_step(j):
        return jnp.maximum(j - n_in, 0)

    out_specs = [pl.BlockSpec((tm, d), lambda i, j: (i, 0))]
    out_shape = [jax.ShapeDtypeStruct((m, d), F32)]
    if emit_v:
        out_specs.append(pl.BlockSpec((tm, width), lambda i, j: (i, 0)))
        out_shape.append(jax.ShapeDtypeStruct((m, width), F32))
    outs = pl.pallas_call(
        functools.partial(_gmlp_kernel, block_diag=block_diag, emit_v=emit_v, n_in=n_in),
        grid=(m // tm, n_in + n_out),
        in_specs=[
            pl.BlockSpec((tm, d), lambda i, j: (i, 0)),
            pl.BlockSpec((1, d), lambda i, j: (0, 0)),
            pl.BlockSpec((d, tn), lambda i, j: (0, in_step(j))),
            pl.BlockSpec((1, tn), lambda i, j: (0, in_step(j))),
            pl.BlockSpec((1, width), lambda i, j: (0, 0)),
            pl.BlockSpec((1, width), lambda i, j: (0, 0)),
            pl.BlockSpec((GMLP_HEADS, GMLP_CHUNK, GMLP_CHUNK), lambda i, j: (0, 0, 0)),
            pl.BlockSpec((GMLP_HEADS, GMLP_CHUNK, 1), lambda i, j: (0, 0, 0)),
            pl.BlockSpec((cw, d), lambda i, j: (out_step(j), 0)),
        ],
        out_specs=out_specs,
        out_shape=out_shape,
        scratch_shapes=[pltpu.VMEM((tm, d), BF16),
                        pltpu.VMEM((2 * n_out, tm, cw), F32),
                        pltpu.VMEM((n_out, tm, cw), BF16)],
        compiler_params=_params(2),
        name="gmlp_v" if emit_v else "gmlp",
    )(x, g, w_in, b_in, ln_g, ln_b, ws, bs, w_out)
    return outs if emit_v else (outs[0], None)


def _merge_col_blocks(state):
    if state.ndim == 3:
        return state
    n_seq, nb, rows, tn = state.shape
    return state.transpose(0, 2, 1, 3).reshape(n_seq, rows, nb * tn)


def _sconv_kernel(x_ref, g_ref, wb_ref, wc_ref, wx_ref, cw_ref, wo_ref, *rest, sample, tiles_per_seq):
    if sample:
        hist_ref, o_ref, st_ref, h_ref, cxe_ref = rest
    else:
        o_ref, st_ref, h_ref, cxe_ref, carry_ref = rest
    i = pl.program_id(0)
    n = pl.program_id(1)
    tm = x_ref.shape[0]
    halo = SCONV_HALO
    first_tap = halo - (SCONV_WIDTH - 1)

    @pl.when(n == 0)
    def _():
        x = x_ref[...]
        h_ref[...] = _rms(x, g_ref[...]).astype(BF16)
        o_ref[...] = x

    h = h_ref[...]
    b_gate = _dot(h, wb_ref[...])
    cx = _dot(h, wc_ref[...]) * _dot(h, wx_ref[...])
    tn = cx.shape[1]

    if sample:
        bt, l = cxe_ref.shape[0], cxe_ref.shape[1] - halo
        cxe_ref[:, 0:halo, :] = hist_ref[...]
        cxe_ref[:, halo:, :] = cx.reshape(bt, l, tn)
        conv = None
        for k in range(SCONV_WIDTH):
            term = cw_ref[k:k + 1, :] * cxe_ref[:, first_tap + k:first_tap + k + l, :]
            conv = term if conv is None else conv + term
        conv = conv.reshape(tm, tn)
        st_ref[...] = cxe_ref[:, l + first_tap:l + halo, :]
    else:
        @pl.when(i % tiles_per_seq == 0)
        def _():
            carry_ref[n] = jnp.zeros((halo, tn), F32)

        cxe_ref[0:halo, :] = carry_ref[n]
        cxe_ref[halo:, :] = cx
        carry_ref[n] = cxe_ref[tm:tm + halo, :]
        conv = None
        for k in range(SCONV_WIDTH):
            term = cw_ref[k:k + 1, :] * cxe_ref[first_tap + k:first_tap + k + tm, :]
            conv = term if conv is None else conv + term
        st_ref[i // tiles_per_seq, n] = cxe_ref[tm + first_tap:tm + halo, :]

    gated = (b_gate * conv).astype(BF16)
    o_ref[...] += _dot(gated, wo_ref[...])


def _sconv(x, g, w_in, conv_w, w_out, hist8, n_seq):
    m, d = x.shape
    sample = hist8 is not None
    tn = COL_TILE
    nb = d // tn
    hist_rows = SCONV_WIDTH - 1
    if sample:
        tm = m
        tiles_per_seq = None
        l = m // n_seq
    else:
        tm = TOKEN_TILE
        tiles_per_seq = (m // n_seq) // tm
    in_specs = [
        pl.BlockSpec((tm, d), lambda i, n: (i, 0)),
        pl.BlockSpec((1, d), lambda i, n: (0, 0)),
        pl.BlockSpec((d, tn), lambda i, n: (0, n)),
        pl.BlockSpec((d, tn), lambda i, n: (0, nb + n)),
        pl.BlockSpec((d, tn), lambda i, n: (0, 2 * nb + n)),
        pl.BlockSpec((SCONV_WIDTH, tn), lambda i, n: (0, n)),
        pl.BlockSpec((tn, d), lambda i, n: (n, 0)),
    ]
    args = [x, g, w_in, w_in, w_in, conv_w, w_out]
    if sample:
        in_specs.append(pl.BlockSpec((n_seq, SCONV_HALO, tn), lambda i, n: (0, 0, n)))
        args.append(hist8)
        st_spec = pl.BlockSpec((n_seq, hist_rows, tn), lambda i, n: (0, 0, n))
        scratch = [pltpu.VMEM((tm, d), BF16), pltpu.VMEM((n_seq, l + SCONV_HALO, tn), F32)]
    else:
        st_spec = pl.BlockSpec((n_seq, nb, hist_rows, tn), lambda i, n: (0, 0, 0, 0))
        scratch = [pltpu.VMEM((tm, d), BF16), pltpu.VMEM((tm + SCONV_HALO, tn), F32),
                   pltpu.VMEM((nb, SCONV_HALO, tn), F32)]
    st_shape = (n_seq, hist_rows, d) if sample else (n_seq, nb, hist_rows, tn)
    out, state = pl.pallas_call(
        functools.partial(_sconv_kernel, sample=sample, tiles_per_seq=tiles_per_seq),
        grid=(m // tm, nb),
        in_specs=in_specs,
        out_specs=[pl.BlockSpec((tm, d), lambda i, n: (i, 0)), st_spec],
        out_shape=[jax.ShapeDtypeStruct((m, d), F32),
                   jax.ShapeDtypeStruct(st_shape, F32)],
        scratch_shapes=scratch,
        compiler_params=_params(2),
        name="sconv_sample" if sample else "sconv_prompt",
    )(*args)
    return out, _merge_col_blocks(state)


def _causal_taps_flat(glue_ref, n, w_ref, c0, tn, tm, halo, width):
    first_tap = halo - (width - 1)
    ext = tm + SUBLANES
    conv = None
    for r in range(SUBLANES):
        part = None
        for o in range(first_tap, first_tap + width):
            if o % SUBLANES != r:
                continue
            k = o - first_tap
            rows = tm if r == 0 else ext
            term = w_ref[k:k + 1, c0:c0 + tn] * glue_ref[n, o - r:o - r + rows, :]
            part = term if part is None else part + term
        if part is None:
            continue
        if r:
            part = pltpu.roll(part, ext - r, axis=0)[:tm]
        conv = part if conv is None else conv + part
    return conv


def _cconv_kernel(x_ref, g_ref, w1_ref, b1_ref, dw_ref, dwb_ref, lng_ref, lnb_ref, w2_ref, b2_ref,
                  *rest, sample, tiles_per_seq, nb):
    if sample:
        hist_ref, o_ref, st_ref, glue_ref, conv_ref = rest
    else:
        o_ref, st_ref, glue_ref, conv_ref, carry_ref = rest
    i = pl.program_id(0)
    tm, d = x_ref.shape
    tn = d // nb
    halo = CCONV_HALO
    first_tap = halo - (CCONV_WIDTH - 1)

    x = x_ref[...]
    h = _rms(x, g_ref[...]).astype(BF16)

    if not sample:
        @pl.when(i % tiles_per_seq == 0)
        def _():
            carry_ref[...] = jnp.zeros_like(carry_ref)

    for n in range(nb):
        c0 = n * tn
        a = _dot(h, w1_ref[n]) + b1_ref[:, c0:c0 + tn]
        gt = _dot(h, w1_ref[nb + n]) + b1_ref[:, d + c0:d + c0 + tn]
        glu = a * jax.nn.sigmoid(gt)
        if sample:
            bt, l = glue_ref.shape[0], glue_ref.shape[1] - halo
            glue_ref[:, 0:halo, :] = hist_ref[:, :, c0:c0 + tn]
            glue_ref[:, halo:, :] = glu.reshape(bt, l, tn)
            conv = None
            for k in range(CCONV_WIDTH):
                term = dw_ref[k:k + 1, c0:c0 + tn] * glue_ref[:, first_tap + k:first_tap + k + l, :]
                conv = term if conv is None else conv + term
            conv = conv.reshape(tm, tn)
            st_ref[:, :, c0:c0 + tn] = glue_ref[:, l + first_tap:l + halo, :]
        else:
            glue_ref[n, 0:halo, :] = carry_ref[n]
            glue_ref[n, halo:, :] = glu
            carry_ref[n] = glue_ref[n, tm:tm + halo, :]
            conv = _causal_taps_flat(glue_ref, n, dw_ref, c0, tn, tm, halo, CCONV_WIDTH)
            st_ref[i // tiles_per_seq, n] = glue_ref[n, tm + first_tap:tm + halo, :]
        conv_ref[:, c0:c0 + tn] = conv + dwb_ref[:, c0:c0 + tn]

    y = _layernorm(conv_ref[...], lng_ref[...], lnb_ref[...])
    z = (y * jax.nn.sigmoid(y)).astype(BF16)
    for n in range(nb):
        cols = slice(n * tn, (n + 1) * tn)
        o_ref[:, cols] = x_ref[:, cols] + _dot(z, w2_ref[n]) + b2_ref[:, cols]


def _col_blocks(w, tn):
    k, n = w.shape
    return w.reshape(k, n // tn, tn).transpose(1, 0, 2)


def _cconv(x, g, w1, b1, dw_w, dw_b, ln_g, ln_b, w2, b2, hist32, n_seq):
    m, d = x.shape
    sample = hist32 is not None
    tn = COL_TILE
    nb = d // tn
    w1, w2 = _col_blocks(w1, tn), _col_blocks(w2, tn)
    hist_rows = CCONV_WIDTH - 1
    tm = CCONV_TOKEN_TILE
    l = m // n_seq
    tiles_per_seq = None if sample else l // tm
    bt = tm // l if sample else None

    def resident(shape):
        return pl.BlockSpec(shape, lambda i: (0,) * len(shape), pipeline_mode=pl.Buffered(1))

    in_specs = [
        pl.BlockSpec((tm, d), lambda i: (i, 0)),
        resident((1, d)),
        resident((2 * nb, d, tn)),
        resident((1, 2 * d)),
        resident((CCONV_WIDTH, d)),
        resident((1, d)),
        resident((1, d)),
        resident((1, d)),
        resident((nb, d, tn)),
        resident((1, d)),
    ]
    args = [x, g, w1, b1, dw_w, dw_b, ln_g, ln_b, w2, b2]
    if sample:
        in_specs.append(pl.BlockSpec((bt, CCONV_HALO, d), lambda i: (i, 0, 0), pipeline_mode=pl.Buffered(1)))
        args.append(hist32)
        st_spec = pl.BlockSpec((bt, hist_rows, d), lambda i: (i, 0, 0))
        scratch = [pltpu.VMEM((bt, l + CCONV_HALO, tn), F32), pltpu.VMEM((tm, d), F32)]
    else:
        st_spec = pl.BlockSpec((n_seq, nb, hist_rows, tn), lambda i: (0, 0, 0, 0))
        scratch = [pltpu.VMEM((nb, tm + CCONV_HALO, tn), F32), pltpu.VMEM((tm, d), F32),
                   pltpu.VMEM((nb, CCONV_HALO, tn), F32)]
    st_shape = (n_seq, hist_rows, d) if sample else (n_seq, nb, hist_rows, tn)
    out, state = pl.pallas_call(
        functools.partial(_cconv_kernel, sample=sample, tiles_per_seq=tiles_per_seq, nb=nb),
        grid=(m // tm,),
        in_specs=in_specs,
        out_specs=[pl.BlockSpec((tm, d), lambda i: (i, 0)), st_spec],
        out_shape=[jax.ShapeDtypeStruct((m, d), F32),
                   jax.ShapeDtypeStruct(st_shape, F32)],
        scratch_shapes=scratch,
        compiler_params=_params(1),
        name="cconv_sample" if sample else "cconv_prompt",
    )(*args)
    return out, _merge_col_blocks(state)


def _pad_front(hist, rows):
    return jnp.pad(hist, ((0, 0), (rows - hist.shape[1], 0), (0, 0)))


def _trunk(x, pool_hist, sconv_hist, cconv_hist, pos0, p, ffn_weights):
    b, s, d = x.shape
    sample = pool_hist is not None
    row = lambda v: v.reshape(1, -1)

    if sample:
        x, pool_state = _pool_sample(x, _pad_front(pool_hist, POOL_HALO), row(p['norm_mix_g'][0]),
                                     p['pool_w'], row(p['pool_scale']), pos0)
    else:
        x, pool_state = _pool_prompt(x, row(p['norm_mix_g'][0]), p['pool_w'], row(p['pool_scale']), pos0)
    x = x.reshape(b * s, d)
    used_weights = []

    def ffn(x, layer, g_final=None):
        if ffn_weights is None:
            w = (p['ffn_w_gate'], p['ffn_w_up'], p['ffn_w_down'])
        else:
            w = ffn_weights[layer]
        out, w_bf16 = _ffn(x, row(p['norm_ffn_g'][layer]), *w, layer, g_final)
        used_weights.append(w_bf16 or w)
        return out

    x = ffn(x, 0)

    if sample:
        reps = GMLP_CHUNK // s
        ws = jnp.tile(p['gmlp_w_s'][:, :s, :s], (1, reps, reps))
        bs = jnp.tile(p['gmlp_b_s'][:, :s], (1, reps))
        block_diag = s
    else:
        ws, bs, block_diag = p['gmlp_w_s'], p['gmlp_b_s'], 0
    x, v = _gmlp(x, row(p['norm_mix_g'][1]), p['gmlp_w_in'], row(p['gmlp_b_in']), row(p['gmlp_ln_g']),
                 row(p['gmlp_ln_b']), ws, bs[:, :, None], p['gmlp_w_out'], block_diag, emit_v=sample)
    x = ffn(x, 1)

    hist8 = _pad_front(sconv_hist, SCONV_HALO) if sample else None
    x, sconv_state = _sconv(x, row(p['norm_mix_g'][2]), p['sconv_w_in'], p['sconv_conv_w'],
                            p['sconv_w_out'], hist8, b)
    x = ffn(x, 2)

    hist32 = _pad_front(cconv_hist, CCONV_HALO) if sample else None
    x, cconv_state = _cconv(x, row(p['norm_mix_g'][3]), p['cconv_w_pw1'], row(p['cconv_b_pw1']),
                            p['cconv_dw_w'], row(p['cconv_dw_b']), row(p['cconv_ln_g']),
                            row(p['cconv_ln_b']), p['cconv_w_pw2'], row(p['cconv_b_pw2']), hist32, b)
    y = ffn(x, 3, row(p['norm_final_g']))
    gmlp_v = v.reshape(b, s, -1) if sample else None
    return y.reshape(b, s, d), pool_state, gmlp_v, sconv_state, cconv_state, used_weights


MIXER_MATMUL_WEIGHTS = ('pool_w', 'gmlp_w_in', 'gmlp_w_out', 'sconv_w_in', 'sconv_w_out',
                        'cconv_w_pw1', 'cconv_w_pw2')


def kernel(x_prompt, x_sample, state_pool, state_sconv, state_cconv, norm_mix_g, norm_ffn_g, norm_final_g, pool_w, pool_scale, gmlp_w_in, gmlp_b_in, gmlp_ln_g, gmlp_ln_b, gmlp_w_s, gmlp_b_s, gmlp_w_out, sconv_w_in, sconv_conv_w, sconv_w_out, cconv_w_pw1, cconv_b_pw1, cconv_dw_w, cconv_dw_b, cconv_ln_g, cconv_ln_b, cconv_w_pw2, cconv_b_pw2, ffn_w_gate, ffn_w_up, ffn_w_down):
    p = dict(norm_mix_g=norm_mix_g, norm_ffn_g=norm_ffn_g, norm_final_g=norm_final_g,
             pool_w=pool_w, pool_scale=pool_scale,
             gmlp_w_in=gmlp_w_in, gmlp_b_in=gmlp_b_in, gmlp_ln_g=gmlp_ln_g, gmlp_ln_b=gmlp_ln_b,
             gmlp_w_s=gmlp_w_s, gmlp_b_s=gmlp_b_s, gmlp_w_out=gmlp_w_out,
             sconv_w_in=sconv_w_in, sconv_conv_w=sconv_conv_w, sconv_w_out=sconv_w_out,
             cconv_w_pw1=cconv_w_pw1, cconv_b_pw1=cconv_b_pw1, cconv_dw_w=cconv_dw_w, cconv_dw_b=cconv_dw_b,
             cconv_ln_g=cconv_ln_g, cconv_ln_b=cconv_ln_b, cconv_w_pw2=cconv_w_pw2, cconv_b_pw2=cconv_b_pw2,
             ffn_w_gate=ffn_w_gate, ffn_w_up=ffn_w_up, ffn_w_down=ffn_w_down)
    for name in MIXER_MATMUL_WEIGHTS:
        p[name] = p[name].astype(BF16)
    y_s, pool_s, gmlp_v_s, sconv_s, cconv_s, ffn_weights = _trunk(
        x_sample, state_pool, state_sconv, state_cconv, PAST_LEN, p, None)
    y_p, pool_p, _, sconv_p, cconv_p, _ = _trunk(x_prompt, None, None, None, 0, p, ffn_weights)
    return (y_p, y_s, pool_p, pool_s, gmlp_v_s, sconv_p, sconv_s, cconv_p, cconv_s)
```

```python
import functools

import jax
import jax.numpy as jnp
from jax import lax
from jax.experimental import pallas as pl
from jax.experimental.pallas import tpu as pltpu

F32 = jnp.float32
BF16 = jnp.bfloat16

EPS = 1e-6
PAST_LEN = 4096
POOL_WINDOWS = (2, 4, 8, 16)
POOL_HIST = max(POOL_WINDOWS) - 1
POOL_HALO = 16
GMLP_CHUNK = 128
GMLP_CAUSAL_SHIFT = 6
GMLP_HEADS = 8
GMLP_HEADS_PER_STEP = 2
GMLP_IN_COL_TILE = 1024
GMLP_EMIT_TOKEN_TILE = 256
SCONV_WIDTH = 3
SCONV_HALO = 8
CCONV_WIDTH = 31
CCONV_HALO = 32

SUBLANES = 8
TOKEN_TILE = 512
CCONV_TOKEN_TILE = 256
FFN_TOKEN_TILE = 1024
FFN_CAST_TOKEN_TILE = 512
FFN_CAST_COL_TILE = 256
COL_TILE = 512
VMEM_LIMIT_BYTES = 56 * 1024 * 1024


def _rms(x, g):
    ms = jnp.mean(x * x, axis=-1, keepdims=True)
    return x * lax.rsqrt(ms + EPS) * g


def _layernorm(x, g, b):
    mu = jnp.mean(x, axis=-1, keepdims=True)
    xc = x - mu
    var = jnp.mean(xc * xc, axis=-1, keepdims=True)
    return xc * lax.rsqrt(var + EPS) * g + b


def _dot(a, b):
    return jnp.dot(a, b, preferred_element_type=F32)


def _params(n_axes):
    return pltpu.CompilerParams(
        dimension_semantics=("arbitrary",) * n_axes,
        vmem_limit_bytes=VMEM_LIMIT_BYTES)


def _ffn_kernel(x_ref, g_ref, wg_ref, wu_ref, wd_ref, *rest, final_norm, emit_bf16):
    rest = list(rest)
    gf_ref = rest.pop(0) if final_norm else None
    o_ref = rest.pop(0)
    i = pl.program_id(0)
    k = pl.program_id(1)
    if emit_bf16:
        h_ref = rest.pop()
        x_tile = x_ref
    else:
        h_ref, x_tile, sem = rest
        tm = x_tile.shape[0]

        def fetch(tile):
            return pltpu.make_async_copy(x_ref.at[pl.ds(tile * tm, tm)], x_tile, sem)

    @pl.when(k == 0)
    def _():
        if not emit_bf16:
            @pl.when(i == 0)
            def _():
                fetch(0).start()

            fetch(i).wait()
        x = x_tile[...]
        h_ref[...] = _rms(x, g_ref[...]).astype(BF16)
        o_ref[...] = x

    wg, wu, wd = wg_ref[...], wu_ref[...], wd_ref[...]
    if emit_bf16:
        wg, wu, wd = wg.astype(BF16), wu.astype(BF16), wd.astype(BF16)
        for w, w_out_ref in zip((wg, wu, wd), rest):
            w_out_ref[...] = w
    h = h_ref[...]
    gate = _dot(h, wg)
    up = _dot(h, wu)
    act = (gate * jax.nn.sigmoid(gate) * up).astype(BF16)
    o_ref[...] += _dot(act, wd)

    if not emit_bf16:
        @pl.when((k == 1) & (i + 1 < pl.num_programs(0)))
        def _():
            fetch(i + 1).start()

    if final_norm:
        @pl.when(k == pl.num_programs(1) - 1)
        def _():
            o_ref[...] = _rms(o_ref[...], gf_ref[...])


def _ffn(x, g, wg, wu, wd, layer, g_final=None):
    m, d = x.shape
    emit_bf16 = wg.ndim == 3
    f = wg.shape[-1]
    if emit_bf16:
        tm, tf = FFN_CAST_TOKEN_TILE, FFN_CAST_COL_TILE
        assert m == tm, "every weight block must be visited exactly once"
        w_specs = [
            pl.BlockSpec((None, d, tf), lambda i, k: (layer, 0, k)),
            pl.BlockSpec((None, d, tf), lambda i, k: (layer, 0, k)),
            pl.BlockSpec((None, tf, d), lambda i, k: (layer, k, 0)),
        ]
    else:
        tm, tf = FFN_TOKEN_TILE, COL_TILE
        w_specs = [
            pl.BlockSpec((d, tf), lambda i, k: (0, k)),
            pl.BlockSpec((d, tf), lambda i, k: (0, k)),
            pl.BlockSpec((tf, d), lambda i, k: (k, 0)),
        ]
    final_norm = g_final is not None
    scratch = [pltpu.VMEM((tm, d), BF16)]
    if emit_bf16:
        x_spec = pl.BlockSpec((tm, d), lambda i, k: (i, 0))
    else:
        x_spec = pl.BlockSpec(memory_space=pl.ANY)
        scratch += [pltpu.VMEM((tm, d), F32), pltpu.SemaphoreType.DMA(())]
    in_specs = [x_spec, pl.BlockSpec((1, d), lambda i, k: (0, 0))] + w_specs
    args = [x, g, wg, wu, wd]
    if final_norm:
        in_specs.append(pl.BlockSpec((1, d), lambda i, k: (0, 0)))
        args.append(g_final)
    out_specs = [pl.BlockSpec((tm, d), lambda i, k: (i, 0))]
    out_shape = [jax.ShapeDtypeStruct((m, d), F32)]
    if emit_bf16:
        out_specs += [pl.BlockSpec((d, tf), lambda i, k: (0, k)),
                      pl.BlockSpec((d, tf), lambda i, k: (0, k)),
                      pl.BlockSpec((tf, d), lambda i, k: (k, 0))]
        out_shape += [jax.ShapeDtypeStruct((d, f), BF16), jax.ShapeDtypeStruct((d, f), BF16),
                      jax.ShapeDtypeStruct((f, d), BF16)]
    outs = pl.pallas_call(
        functools.partial(_ffn_kernel, final_norm=final_norm, emit_bf16=emit_bf16),
        grid=(m // tm, f // tf),
        in_specs=in_specs,
        out_specs=out_specs,
        out_shape=out_shape,
        scratch_shapes=scratch,
        compiler_params=_params(2),
        name=("ffn_cast" if emit_bf16 else "ffn") + ("_final" if final_norm else ""),
    )(*args)
    return outs[0], tuple(outs[1:])


def _pool_body(x_ref, hist, g_ref, pw_ref, sc_ref, o_ref, xe_ref, pos_first):
    bt, l, d = x_ref.shape
    grp = d // len(POOL_WINDOWS)
    x = x_ref[...]
    xe_ref[:, 0:POOL_HALO, :] = hist
    xe_ref[:, POOL_HALO:, :] = _rms(x, g_ref[...])
    pos = lax.broadcasted_iota(jnp.int32, (1, l, 1), 1) + pos_first
    for gi, w in enumerate(POOL_WINDOWS):
        c0 = gi * grp
        hg = xe_ref[:, POOL_HALO:, c0:c0 + grp]
        if bt == 1:
            a = xe_ref[0, :, c0:c0 + grp]
            span = 1
            while span < min(w, SUBLANES):
                a = a + pltpu.roll(a, span, axis=0)
                span *= 2
            acc = a[POOL_HALO:POOL_HALO + l]
            if w > SUBLANES:
                assert w == 2 * SUBLANES
                acc = acc + a[POOL_HALO - SUBLANES:POOL_HALO - SUBLANES + l]
            acc = acc[None]
        else:
            acc = hg
            for k in range(1, w):
                acc = acc + xe_ref[:, POOL_HALO - k:POOL_HALO - k + l, c0:c0 + grp]
        inv_cnt = 1.0 / jnp.minimum(pos + 1, w).astype(F32)
        diff = (acc * inv_cnt - hg).reshape(bt * l, grp).astype(BF16)
        y = _dot(diff, pw_ref[gi]) * sc_ref[:, c0:c0 + grp]
        o_ref[:, :, c0:c0 + grp] = x_ref[:, :, c0:c0 + grp] + y.reshape(bt, l, grp)


def _pool_prompt_kernel(x_ref, g_ref, pw_ref, sc_ref, o_ref, st_ref, xe_ref, carry_ref, *, pos0):
    i = pl.program_id(1)
    l = x_ref.shape[1]

    @pl.when(i == 0)
    def _():
        carry_ref[...] = jnp.zeros_like(carry_ref)

    _pool_body(x_ref, carry_ref[...], g_ref, pw_ref, sc_ref, o_ref, xe_ref, pos0 + i * l)
    carry_ref[...] = xe_ref[:, l:l + POOL_HALO, :]
    st_ref[...] = xe_ref[:, l + POOL_HALO - POOL_HIST:l + POOL_HALO, :]


def _pool_sample_kernel(x_ref, hist_ref, g_ref, pw_ref, sc_ref, o_ref, st_ref, xe_ref, *, pos0):
    l = x_ref.shape[1]
    _pool_body(x_ref, hist_ref[...], g_ref, pw_ref, sc_ref, o_ref, xe_ref, pos0)
    st_ref[...] = xe_ref[:, l + POOL_HALO - POOL_HIST:l + POOL_HALO, :]


def _pool_prompt(x, g, pw, scale, pos0):
    b, s, d = x.shape
    ts = TOKEN_TILE
    ng = len(POOL_WINDOWS)
    return pl.pallas_call(
        functools.partial(_pool_prompt_kernel, pos0=pos0),
        grid=(b, s // ts),
        in_specs=[
            pl.BlockSpec((1, ts, d), lambda bi, i: (bi, i, 0)),
            pl.BlockSpec((1, d), lambda bi, i: (0, 0)),
            pl.BlockSpec((ng, d // ng, d // ng), lambda bi, i: (0, 0, 0)),
            pl.BlockSpec((1, d), lambda bi, i: (0, 0)),
        ],
        out_specs=[
            pl.BlockSpec((1, ts, d), lambda bi, i: (bi, i, 0)),
            pl.BlockSpec((1, POOL_HIST, d), lambda bi, i: (bi, 0, 0)),
        ],
        out_shape=[jax.ShapeDtypeStruct((b, s, d), F32),
                   jax.ShapeDtypeStruct((b, POOL_HIST, d), F32)],
        scratch_shapes=[pltpu.VMEM((1, ts + POOL_HALO, d), F32),
                        pltpu.VMEM((1, POOL_HALO, d), F32)],
        compiler_params=_params(2),
        name="pool_prompt",
    )(x, g, pw, scale)


def _pool_sample(x, hist16, g, pw, scale, pos0):
    b, s, d = x.shape
    bt = 8
    ng = len(POOL_WINDOWS)
    return pl.pallas_call(
        functools.partial(_pool_sample_kernel, pos0=pos0),
        grid=(b // bt,),
        in_specs=[
            pl.BlockSpec((bt, s, d), lambda i: (i, 0, 0)),
            pl.BlockSpec((bt, POOL_HALO, d), lambda i: (i, 0, 0)),
            pl.BlockSpec((1, d), lambda i: (0, 0)),
            pl.BlockSpec((ng, d // ng, d // ng), lambda i: (0, 0, 0)),
            pl.BlockSpec((1, d), lambda i: (0, 0)),
        ],
        out_specs=[
            pl.BlockSpec((bt, s, d), lambda i: (i, 0, 0)),
            pl.BlockSpec((bt, POOL_HIST, d), lambda i: (i, 0, 0)),
        ],
        out_shape=[jax.ShapeDtypeStruct((b, s, d), F32),
                   jax.ShapeDtypeStruct((b, POOL_HIST, d), F32)],
        scratch_shapes=[pltpu.VMEM((bt, s + POOL_HALO, d), F32)],
        compiler_params=_params(1),
        name="pool_sample",
    )(x, hist16, g, pw, scale)


def _gmlp_kernel(x_ref, g_ref, win_ref, bin_ref, lng_ref, lnb_ref, ws_ref, bs_ref, wout_ref, o_ref, *rest,
                 block_diag, emit_v, n_in):
    if emit_v:
        vout_ref, h_ref, z_ref, v_ref = rest
    else:
        h_ref, z_ref, v_ref = rest
    j = pl.program_id(1)
    tm = x_ref.shape[0]
    cw = z_ref.shape[2]
    n_slots = z_ref.shape[0]
    hdim = cw // GMLP_HEADS_PER_STEP

    @pl.when(j == 0)
    def _():
        x = x_ref[...]
        h_ref[...] = _rms(x, g_ref[...]).astype(BF16)
        o_ref[...] = x

    @pl.when(j < n_in)
    def _():
        z = jax.nn.gelu(_dot(h_ref[...], win_ref[...]) + bin_ref[...])
        per_step = z.shape[1] // cw
        for q in range(per_step):
            z_ref[j * per_step + q] = z[:, q * cw:(q + 1) * cw]

    @pl.when(j == n_in - 1)
    def _():
        half = n_slots // 2
        v = _layernorm(jnp.concatenate([z_ref[half + q] for q in range(half)], axis=1),
                       lng_ref[...], lnb_ref[...])
        if emit_v:
            vout_ref[...] = v
        for q in range(half):
            v_ref[q] = v[:, q * cw:(q + 1) * cw].astype(BF16)

    @pl.when(j >= n_in)
    def _():
        q = j - n_in
        row = lax.broadcasted_iota(jnp.int32, (GMLP_CHUNK, GMLP_CHUNK), 0)
        col = lax.broadcasted_iota(jnp.int32, (GMLP_CHUNK, GMLP_CHUNK), 1)
        if block_diag:
            shift = block_diag.bit_length() - 1
            mask = (row >> shift) == (col >> shift)
        else:
            mask = (col >> GMLP_CAUSAL_SHIFT) <= (row >> GMLP_CAUSAL_SHIFT)
        cols = []
        for hh in range(GMLP_HEADS_PER_STEP):
            head = q * GMLP_HEADS_PER_STEP + hh
            wm = jnp.where(mask, ws_ref[head], 0.0).astype(BF16)
            bias = bs_ref[head]
            lanes = slice(hh * hdim, (hh + 1) * hdim)
            parts = []
            for c in range(tm // GMLP_CHUNK):
                rows = slice(c * GMLP_CHUNK, (c + 1) * GMLP_CHUNK)
                s = _dot(wm, v_ref[q, rows, lanes]) + bias
                parts.append((z_ref[q, rows, lanes] * s).astype(BF16))
            cols.append(jnp.concatenate(parts, axis=0))
        t = jnp.concatenate(cols, axis=1)
        o_ref[...] += _dot(t, wout_ref[...])


def _gmlp(x, g, w_in, b_in, ln_g, ln_b, ws, bs, w_out, block_diag, emit_v):
    m, d = x.shape
    n = w_in.shape[1]
    width = n // 2
    tm = GMLP_EMIT_TOKEN_TILE if emit_v else TOKEN_TILE
    tn = GMLP_IN_COL_TILE
    n_in = n // tn
    cw = GMLP_HEADS_PER_STEP * (width // GMLP_HEADS)
    n_out = width // cw

    def in_step(j):
        return jnp.minimum(j, n_in - 1)

    def out_step(j):
        return jnp.maximum(j - n_in, 0)

    out_specs = [pl.BlockSpec((tm, d), lambda i, j: (i, 0))]
    out_shape = [jax.ShapeDtypeStruct((m, d), F32)]
    if emit_v:
        out_specs.append(pl.BlockSpec((tm, width), lambda i, j: (i, 0)))
        out_shape.append(jax.ShapeDtypeStruct((m, width), F32))
    outs = pl.pallas_call(
        functools.partial(_gmlp_kernel, block_diag=block_diag, emit_v=emit_v, n_in=n_in),
        grid=(m // tm, n_in + n_out),
        in_specs=[
            pl.BlockSpec((tm, d), lambda i, j: (i, 0)),
            pl.BlockSpec((1, d), lambda i, j: (0, 0)),
            pl.BlockSpec((d, tn), lambda i, j: (0, in_step(j))),
            pl.BlockSpec((1, tn), lambda i, j: (0, in_step(j))),
            pl.BlockSpec((1, width), lambda i, j: (0, 0)),
            pl.BlockSpec((1, width), lambda i, j: (0, 0)),
            pl.BlockSpec((GMLP_HEADS, GMLP_CHUNK, GMLP_CHUNK), lambda i, j: (0, 0, 0)),
            pl.BlockSpec((GMLP_HEADS, GMLP_CHUNK, 1), lambda i, j: (0, 0, 0)),
            pl.BlockSpec((cw, d), lambda i, j: (out_step(j), 0)),
        ],
        out_specs=out_specs,
        out_shape=out_shape,
        scratch_shapes=[pltpu.VMEM((tm, d), BF16),
                        pltpu.VMEM((2 * n_out, tm, cw), F32),
                        pltpu.VMEM((n_out, tm, cw), BF16)],
        compiler_params=_params(2),
        name="gmlp_v" if emit_v else "gmlp",
    )(x, g, w_in, b_in, ln_g, ln_b, ws, bs, w_out)
    return outs if emit_v else (outs[0], None)


def _merge_col_blocks(state):
    if state.ndim == 3:
        return state
    n_seq, nb, rows, tn = state.shape
    return state.transpose(0, 2, 1, 3).reshape(n_seq, rows, nb * tn)


def _sconv_kernel(x_ref, g_ref, wb_ref, wc_ref, wx_ref, cw_ref, wo_ref, *rest, sample, tiles_per_seq):
    if sample:
        hist_ref, o_ref, st_ref, h_ref, cxe_ref = rest
    else:
        o_ref, st_ref, h_ref, cxe_ref, carry_ref = rest
    i = pl.program_id(0)
    n = pl.program_id(1)
    tm = x_ref.shape[0]
    halo = SCONV_HALO
    first_tap = halo - (SCONV_WIDTH - 1)

    @pl.when(n == 0)
    def _():
        x = x_ref[...]
        h_ref[...] = _rms(x, g_ref[...]).astype(BF16)
        o_ref[...] = x

    h = h_ref[...]
    b_gate = _dot(h, wb_ref[...])
    cx = _dot(h, wc_ref[...]) * _dot(h, wx_ref[...])
    tn = cx.shape[1]

    if sample:
        bt, l = cxe_ref.shape[0], cxe_ref.shape[1] - halo
        cxe_ref[:, 0:halo, :] = hist_ref[...]
        cxe_ref[:, halo:, :] = cx.reshape(bt, l, tn)
        conv = None
        for k in range(SCONV_WIDTH):
            term = cw_ref[k:k + 1, :] * cxe_ref[:, first_tap + k:first_tap + k + l, :]
            conv = term if conv is None else conv + term
        conv = conv.reshape(tm, tn)
        st_ref[...] = cxe_ref[:, l + first_tap:l + halo, :]
    else:
        @pl.when(i % tiles_per_seq == 0)
        def _():
            carry_ref[n] = jnp.zeros((halo, tn), F32)

        cxe_ref[0:halo, :] = carry_ref[n]
        cxe_ref[halo:, :] = cx
        carry_ref[n] = cxe_ref[tm:tm + halo, :]
        conv = None
        for k in range(SCONV_WIDTH):
            term = cw_ref[k:k + 1, :] * cxe_ref[first_tap + k:first_tap + k + tm, :]
            conv = term if conv is None else conv + term
        st_ref[i // tiles_per_seq, n] = cxe_ref[tm + first_tap:tm + halo, :]

    gated = (b_gate * conv).astype(BF16)
    o_ref[...] += _dot(gated, wo_ref[...])


def _sconv(x, g, w_in, conv_w, w_out, hist8, n_seq):
    m, d = x.shape
    sample = hist8 is not None
    tn = COL_TILE
    nb = d // tn
    hist_rows = SCONV_WIDTH - 1
    if sample:
        tm = m
        tiles_per_seq = None
        l = m // n_seq
    else:
        tm = TOKEN_TILE
        tiles_per_seq = (m // n_seq) // tm
    in_specs = [
        pl.BlockSpec((tm, d), lambda i, n: (i, 0)),
        pl.BlockSpec((1, d), lambda i, n: (0, 0)),
        pl.BlockSpec((d, tn), lambda i, n: (0, n)),
        pl.BlockSpec((d, tn), lambda i, n: (0, nb + n)),
        pl.BlockSpec((d, tn), lambda i, n: (0, 2 * nb + n)),
        pl.BlockSpec((SCONV_WIDTH, tn), lambda i, n: (0, n)),
        pl.BlockSpec((tn, d), lambda i, n: (n, 0)),
    ]
    args = [x, g, w_in, w_in, w_in, conv_w, w_out]
    if sample:
        in_specs.append(pl.BlockSpec((n_seq, SCONV_HALO, tn), lambda i, n: (0, 0, n)))
        args.append(hist8)
        st_spec = pl.BlockSpec((n_seq, hist_rows, tn), lambda i, n: (0, 0, n))
        scratch = [pltpu.VMEM((tm, d), BF16), pltpu.VMEM((n_seq, l + SCONV_HALO, tn), F32)]
    else:
        st_spec = pl.BlockSpec((n_seq, nb, hist_rows, tn), lambda i, n: (0, 0, 0, 0))
        scratch = [pltpu.VMEM((tm, d), BF16), pltpu.VMEM((tm + SCONV_HALO, tn), F32),
                   pltpu.VMEM((nb, SCONV_HALO, tn), F32)]
    st_shape = (n_seq, hist_rows, d) if sample else (n_seq, nb, hist_rows, tn)
    out, state = pl.pallas_call(
        functools.partial(_sconv_kernel, sample=sample, tiles_per_seq=tiles_per_seq),
        grid=(m // tm, nb),
        in_specs=in_specs,
        out_specs=[pl.BlockSpec((tm, d), lambda i, n: (i, 0)), st_spec],
        out_shape=[jax.ShapeDtypeStruct((m, d), F32),
                   jax.ShapeDtypeStruct(st_shape, F32)],
        scratch_shapes=scratch,
        compiler_params=_params(2),
        name="sconv_sample" if sample else "sconv_prompt",
    )(*args)
    return out, _merge_col_blocks(state)


def _causal_taps_flat(glue_ref, n, w_ref, c0, tn, tm, halo, width):
    first_tap = halo - (width - 1)
    ext = tm + SUBLANES
    conv = None
    for r in range(SUBLANES):
        part = None
        for o in range(first_tap, first_tap + width):
            if o % SUBLANES != r:
                continue
            k = o - first_tap
            rows = tm if r == 0 else ext
            term = w_ref[k:k + 1, c0:c0 + tn] * glue_ref[n, o - r:o - r + rows, :]
            part = term if part is None else part + term
        if part is None:
            continue
        if r:
            part = pltpu.roll(part, ext - r, axis=0)[:tm]
        conv = part if conv is None else conv + part
    return conv


def _cconv_kernel(x_ref, g_ref, w1_ref, b1_ref, dw_ref, dwb_ref, lng_ref, lnb_ref, w2_ref, b2_ref,
                  *rest, sample, tiles_per_seq, nb):
    if sample:
        hist_ref, o_ref, st_ref, glue_ref, conv_ref = rest
    else:
        o_ref, st_ref, glue_ref, conv_ref, carry_ref = rest
    i = pl.program_id(0)
    tm, d = x_ref.shape
    tn = d // nb
    halo = CCONV_HALO
    first_tap = halo - (CCONV_WIDTH - 1)

    x = x_ref[...]
    h = _rms(x, g_ref[...]).astype(BF16)

    if not sample:
        @pl.when(i % tiles_per_seq == 0)
        def _():
            carry_ref[...] = jnp.zeros_like(carry_ref)

    for n in range(nb):
        c0 = n * tn
        a = _dot(h, w1_ref[n]) + b1_ref[:, c0:c0 + tn]
        gt = _dot(h, w1_ref[nb + n]) + b1_ref[:, d + c0:d + c0 + tn]
        glu = a * jax.nn.sigmoid(gt)
        if sample:
            bt, l = glue_ref.shape[0], glue_ref.shape[1] - halo
            glue_ref[:, 0:halo, :] = hist_ref[:, :, c0:c0 + tn]
            glue_ref[:, halo:, :] = glu.reshape(bt, l, tn)
            conv = None
            for k in range(CCONV_WIDTH):
                term = dw_ref[k:k + 1, c0:c0 + tn] * glue_ref[:, first_tap + k:first_tap + k + l, :]
                conv = term if conv is None else conv + term
            conv = conv.reshape(tm, tn)
            st_ref[:, :, c0:c0 + tn] = glue_ref[:, l + first_tap:l + halo, :]
        else:
            glue_ref[n, 0:halo, :] = carry_ref[n]
            glue_ref[n, halo:, :] = glu
            carry_ref[n] = glue_ref[n, tm:tm + halo, :]
            conv = _causal_taps_flat(glue_ref, n, dw_ref, c0, tn, tm, halo, CCONV_WIDTH)
            st_ref[i // tiles_per_seq, n] = glue_ref[n, tm + first_tap:tm + halo, :]
        conv_ref[:, c0:c0 + tn] = conv + dwb_ref[:, c0:c0 + tn]

    y = _layernorm(conv_ref[...], lng_ref[...], lnb_ref[...])
    z = (y * jax.nn.sigmoid(y)).astype(BF16)
    for n in range(nb):
        cols = slice(n * tn, (n + 1) * tn)
        o_ref[:, cols] = x_ref[:, cols] + _dot(z, w2_ref[n]) + b2_ref[:, cols]


def _cast_kernel(w_ref, o_ref):
    o_ref[...] = w_ref[...].astype(o_ref.dtype)


def _col_blocks_bf16(w, tn):
    k, n = w.shape
    return pl.pallas_call(
        _cast_kernel,
        grid=(n // tn,),
        in_specs=[pl.BlockSpec((k, tn), lambda j: (0, j))],
        out_specs=pl.BlockSpec((None, k, tn), lambda j: (j, 0, 0)),
        out_shape=jax.ShapeDtypeStruct((n // tn, k, tn), BF16),
        compiler_params=_params(1),
        name="cast_col_blocks",
    )(w)


def _cconv(x, g, w1, b1, dw_w, dw_b, ln_g, ln_b, w2, b2, hist32, n_seq):
    m, d = x.shape
    sample = hist32 is not None
    nb, _, tn = w2.shape
    hist_rows = CCONV_WIDTH - 1
    tm = CCONV_TOKEN_TILE
    l = m // n_seq
    tiles_per_seq = None if sample else l // tm
    bt = tm // l if sample else None

    def resident(shape):
        return pl.BlockSpec(shape, lambda i: (0,) * len(shape), pipeline_mode=pl.Buffered(1))

    in_specs = [
        pl.BlockSpec((tm, d), lambda i: (i, 0)),
        resident((1, d)),
        resident((2 * nb, d, tn)),
        resident((1, 2 * d)),
        resident((CCONV_WIDTH, d)),
        resident((1, d)),
        resident((1, d)),
        resident((1, d)),
        resident((nb, d, tn)),
        resident((1, d)),
    ]
    args = [x, g, w1, b1, dw_w, dw_b, ln_g, ln_b, w2, b2]
    if sample:
        in_specs.append(pl.BlockSpec((bt, CCONV_HALO, d), lambda i: (i, 0, 0), pipeline_mode=pl.Buffered(1)))
        args.append(hist32)
        st_spec = pl.BlockSpec((bt, hist_rows, d), lambda i: (i, 0, 0))
        scratch = [pltpu.VMEM((bt, l + CCONV_HALO, tn), F32), pltpu.VMEM((tm, d), F32)]
    else:
        st_spec = pl.BlockSpec((n_seq, nb, hist_rows, tn), lambda i: (0, 0, 0, 0))
        scratch = [pltpu.VMEM((nb, tm + CCONV_HALO, tn), F32), pltpu.VMEM((tm, d), F32),
                   pltpu.VMEM((nb, CCONV_HALO, tn), F32)]
    st_shape = (n_seq, hist_rows, d) if sample else (n_seq, nb, hist_rows, tn)
    out, state = pl.pallas_call(
        functools.partial(_cconv_kernel, sample=sample, tiles_per_seq=tiles_per_seq, nb=nb),
        grid=(m // tm,),
        in_specs=in_specs,
        out_specs=[pl.BlockSpec((tm, d), lambda i: (i, 0)), st_spec],
        out_shape=[jax.ShapeDtypeStruct((m, d), F32),
                   jax.ShapeDtypeStruct(st_shape, F32)],
        scratch_shapes=scratch,
        compiler_params=_params(1),
        name="cconv_sample" if sample else "cconv_prompt",
    )(*args)
    return out, _merge_col_blocks(state)


def _pad_front(hist, rows):
    return jnp.pad(hist, ((0, 0), (rows - hist.shape[1], 0), (0, 0)))


def _trunk(x, pool_hist, sconv_hist, cconv_hist, pos0, p, ffn_weights):
    b, s, d = x.shape
    sample = pool_hist is not None
    row = lambda v: v.reshape(1, -1)

    if sample:
        x, pool_state = _pool_sample(x, _pad_front(pool_hist, POOL_HALO), row(p['norm_mix_g'][0]),
                                     p['pool_w'], row(p['pool_scale']), pos0)
    else:
        x, pool_state = _pool_prompt(x, row(p['norm_mix_g'][0]), p['pool_w'], row(p['pool_scale']), pos0)
    x = x.reshape(b * s, d)
    used_weights = []

    def ffn(x, layer, g_final=None):
        if ffn_weights is None:
            w = (p['ffn_w_gate'], p['ffn_w_up'], p['ffn_w_down'])
        else:
            w = ffn_weights[layer]
        out, w_bf16 = _ffn(x, row(p['norm_ffn_g'][layer]), *w, layer, g_final)
        used_weights.append(w_bf16 or w)
        return out

    x = ffn(x, 0)

    if sample:
        reps = GMLP_CHUNK // s
        ws = jnp.tile(p['gmlp_w_s'][:, :s, :s], (1, reps, reps))
        bs = jnp.tile(p['gmlp_b_s'][:, :s], (1, reps))
        block_diag = s
    else:
        ws, bs, block_diag = p['gmlp_w_s'], p['gmlp_b_s'], 0
    x, v = _gmlp(x, row(p['norm_mix_g'][1]), p['gmlp_w_in'], row(p['gmlp_b_in']), row(p['gmlp_ln_g']),
                 row(p['gmlp_ln_b']), ws, bs[:, :, None], p['gmlp_w_out'], block_diag, emit_v=sample)
    x = ffn(x, 1)

    hist8 = _pad_front(sconv_hist, SCONV_HALO) if sample else None
    x, sconv_state = _sconv(x, row(p['norm_mix_g'][2]), p['sconv_w_in'], p['sconv_conv_w'],
                            p['sconv_w_out'], hist8, b)
    x = ffn(x, 2)

    hist32 = _pad_front(cconv_hist, CCONV_HALO) if sample else None
    x, cconv_state = _cconv(x, row(p['norm_mix_g'][3]), p['cconv_w_pw1'], row(p['cconv_b_pw1']),
                            p['cconv_dw_w'], row(p['cconv_dw_b']), row(p['cconv_ln_g']),
                            row(p['cconv_ln_b']), p['cconv_w_pw2'], row(p['cconv_b_pw2']), hist32, b)
    y = ffn(x, 3, row(p['norm_final_g']))
    gmlp_v = v.reshape(b, s, -1) if sample else None
    return y.reshape(b, s, d), pool_state, gmlp_v, sconv_state, cconv_state, used_weights


MIXER_MATMUL_WEIGHTS = ('pool_w', 'gmlp_w_in', 'gmlp_w_out', 'sconv_w_in', 'sconv_w_out')
COL_BLOCKED_WEIGHTS = ('cconv_w_pw1', 'cconv_w_pw2')


def kernel(x_prompt, x_sample, state_pool, state_sconv, state_cconv, norm_mix_g, norm_ffn_g, norm_final_g, pool_w, pool_scale, gmlp_w_in, gmlp_b_in, gmlp_ln_g, gmlp_ln_b, gmlp_w_s, gmlp_b_s, gmlp_w_out, sconv_w_in, sconv_conv_w, sconv_w_out, cconv_w_pw1, cconv_b_pw1, cconv_dw_w, cconv_dw_b, cconv_ln_g, cconv_ln_b, cconv_w_pw2, cconv_b_pw2, ffn_w_gate, ffn_w_up, ffn_w_down):
    p = dict(norm_mix_g=norm_mix_g, norm_ffn_g=norm_ffn_g, norm_final_g=norm_final_g,
             pool_w=pool_w, pool_scale=pool_scale,
             gmlp_w_in=gmlp_w_in, gmlp_b_in=gmlp_b_in, gmlp_ln_g=gmlp_ln_g, gmlp_ln_b=gmlp_ln_b,
             gmlp_w_s=gmlp_w_s, gmlp_b_s=gmlp_b_s, gmlp_w_out=gmlp_w_out,
             sconv_w_in=sconv_w_in, sconv_conv_w=sconv_conv_w, sconv_w_out=sconv_w_out,
             cconv_w_pw1=cconv_w_pw1, cconv_b_pw1=cconv_b_pw1, cconv_dw_w=cconv_dw_w, cconv_dw_b=cconv_dw_b,
             cconv_ln_g=cconv_ln_g, cconv_ln_b=cconv_ln_b, cconv_w_pw2=cconv_w_pw2, cconv_b_pw2=cconv_b_pw2,
             ffn_w_gate=ffn_w_gate, ffn_w_up=ffn_w_up, ffn_w_down=ffn_w_down)
    for name in MIXER_MATMUL_WEIGHTS:
        p[name] = p[name].astype(BF16)
    for name in COL_BLOCKED_WEIGHTS:
        p[name] = _col_blocks_bf16(p[name], COL_TILE)
    y_s, pool_s, gmlp_v_s, sconv_s, cconv_s, ffn_weights = _trunk(
        x_sample, state_pool, state_sconv, state_cconv, PAST_LEN, p, None)
    y_p, pool_p, _, sconv_p, cconv_p, _ = _trunk(x_prompt, None, None, None, 0, p, ffn_weights)
    return (y_p, y_s, pool_p, pool_s, gmlp_v_s, sconv_p, sconv_s, cconv_p, cconv_s)
```

```python
import functools

import jax
import jax.numpy as jnp
from jax import lax
from jax.experimental import pallas as pl
from jax.experimental.pallas import tpu as pltpu

F32 = jnp.float32
BF16 = jnp.bfloat16

EPS = 1e-6
PAST_LEN = 4096
POOL_WINDOWS = (2, 4, 8, 16)
POOL_HIST = max(POOL_WINDOWS) - 1
POOL_HALO = 16
GMLP_CHUNK = 128
GMLP_CAUSAL_SHIFT = 6
GMLP_HEADS = 8
GMLP_TOKEN_TILE = 256
SCONV_TOKEN_TILE = 256
SCONV_WIDTH = 3
SCONV_HALO = 8
CCONV_WIDTH = 31
CCONV_HALO = 32

SUBLANES = 8
TOKEN_TILE = 512
CCONV_TOKEN_TILE = 256
FFN_TOKEN_TILE = 1024
FFN_CAST_TOKEN_TILE = 512
FFN_CAST_COL_TILE = 256
COL_TILE = 512
VMEM_LIMIT_BYTES = 56 * 1024 * 1024


def _rms(x, g):
    ms = jnp.mean(x * x, axis=-1, keepdims=True)
    return x * lax.rsqrt(ms + EPS) * g


def _layernorm(x, g, b):
    mu = jnp.mean(x, axis=-1, keepdims=True)
    xc = x - mu
    var = jnp.mean(xc * xc, axis=-1, keepdims=True)
    return xc * lax.rsqrt(var + EPS) * g + b


def _dot(a, b):
    return jnp.dot(a, b, preferred_element_type=F32)


def _params(n_axes):
    return pltpu.CompilerParams(
        dimension_semantics=("arbitrary",) * n_axes,
        vmem_limit_bytes=VMEM_LIMIT_BYTES)


def _ffn_kernel(x_ref, g_ref, wg_ref, wu_ref, wd_ref, *rest, final_norm, emit_bf16):
    rest = list(rest)
    gf_ref = rest.pop(0) if final_norm else None
    o_ref = rest.pop(0)
    i = pl.program_id(0)
    k = pl.program_id(1)
    if emit_bf16:
        h_ref = rest.pop()
        x_tile = x_ref
    else:
        h_ref, x_tile, sem = rest
        tm = x_tile.shape[0]

        def fetch(tile):
            return pltpu.make_async_copy(x_ref.at[pl.ds(tile * tm, tm)], x_tile, sem)

    @pl.when(k == 0)
    def _():
        if not emit_bf16:
            @pl.when(i == 0)
            def _():
                fetch(0).start()

            fetch(i).wait()
        x = x_tile[...]
        h_ref[...] = _rms(x, g_ref[...]).astype(BF16)
        o_ref[...] = x

    wg, wu, wd = wg_ref[...], wu_ref[...], wd_ref[...]
    if emit_bf16:
        wg, wu, wd = wg.astype(BF16), wu.astype(BF16), wd.astype(BF16)
        for w, w_out_ref in zip((wg, wu, wd), rest):
            w_out_ref[...] = w
    h = h_ref[...]
    gate = _dot(h, wg)
    up = _dot(h, wu)
    act = (gate * jax.nn.sigmoid(gate) * up).astype(BF16)
    o_ref[...] += _dot(act, wd)

    if not emit_bf16:
        @pl.when((k == 1) & (i + 1 < pl.num_programs(0)))
        def _():
            fetch(i + 1).start()

    if final_norm:
        @pl.when(k == pl.num_programs(1) - 1)
        def _():
            o_ref[...] = _rms(o_ref[...], gf_ref[...])


def _ffn(x, g, wg, wu, wd, layer, g_final=None):
    m, d = x.shape
    emit_bf16 = wg.ndim == 3
    f = wg.shape[-1]
    if emit_bf16:
        tm, tf = FFN_CAST_TOKEN_TILE, FFN_CAST_COL_TILE
        assert m == tm, "every weight block must be visited exactly once"
        w_specs = [
            pl.BlockSpec((None, d, tf), lambda i, k: (layer, 0, k)),
            pl.BlockSpec((None, d, tf), lambda i, k: (layer, 0, k)),
            pl.BlockSpec((None, tf, d), lambda i, k: (layer, k, 0)),
        ]
    else:
        tm, tf = FFN_TOKEN_TILE, COL_TILE
        w_specs = [
            pl.BlockSpec((d, tf), lambda i, k: (0, k)),
            pl.BlockSpec((d, tf), lambda i, k: (0, k)),
            pl.BlockSpec((tf, d), lambda i, k: (k, 0)),
        ]
    final_norm = g_final is not None
    scratch = [pltpu.VMEM((tm, d), BF16)]
    if emit_bf16:
        x_spec = pl.BlockSpec((tm, d), lambda i, k: (i, 0))
    else:
        x_spec = pl.BlockSpec(memory_space=pl.ANY)
        scratch += [pltpu.VMEM((tm, d), F32), pltpu.SemaphoreType.DMA(())]
    in_specs = [x_spec, pl.BlockSpec((1, d), lambda i, k: (0, 0))] + w_specs
    args = [x, g, wg, wu, wd]
    if final_norm:
        in_specs.append(pl.BlockSpec((1, d), lambda i, k: (0, 0)))
        args.append(g_final)
    out_specs = [pl.BlockSpec((tm, d), lambda i, k: (i, 0))]
    out_shape = [jax.ShapeDtypeStruct((m, d), F32)]
    if emit_bf16:
        out_specs += [pl.BlockSpec((d, tf), lambda i, k: (0, k)),
                      pl.BlockSpec((d, tf), lambda i, k: (0, k)),
                      pl.BlockSpec((tf, d), lambda i, k: (k, 0))]
        out_shape += [jax.ShapeDtypeStruct((d, f), BF16), jax.ShapeDtypeStruct((d, f), BF16),
                      jax.ShapeDtypeStruct((f, d), BF16)]
    outs = pl.pallas_call(
        functools.partial(_ffn_kernel, final_norm=final_norm, emit_bf16=emit_bf16),
        grid=(m // tm, f // tf),
        in_specs=in_specs,
        out_specs=out_specs,
        out_shape=out_shape,
        scratch_shapes=scratch,
        compiler_params=_params(2),
        name=("ffn_cast" if emit_bf16 else "ffn") + ("_final" if final_norm else ""),
    )(*args)
    return outs[0], tuple(outs[1:])


def _pool_body(x_ref, hist, g_ref, pw_ref, sc_ref, o_ref, xe_ref, pos_first):
    bt, l, d = x_ref.shape
    grp = d // len(POOL_WINDOWS)
    x = x_ref[...]
    xe_ref[:, 0:POOL_HALO, :] = hist
    xe_ref[:, POOL_HALO:, :] = _rms(x, g_ref[...])
    pos = lax.broadcasted_iota(jnp.int32, (1, l, 1), 1) + pos_first
    for gi, w in enumerate(POOL_WINDOWS):
        c0 = gi * grp
        hg = xe_ref[:, POOL_HALO:, c0:c0 + grp]
        if bt == 1:
            a = xe_ref[0, :, c0:c0 + grp]
            span = 1
            while span < min(w, SUBLANES):
                a = a + pltpu.roll(a, span, axis=0)
                span *= 2
            acc = a[POOL_HALO:POOL_HALO + l]
            if w > SUBLANES:
                assert w == 2 * SUBLANES
                acc = acc + a[POOL_HALO - SUBLANES:POOL_HALO - SUBLANES + l]
            acc = acc[None]
        else:
            acc = hg
            for k in range(1, w):
                acc = acc + xe_ref[:, POOL_HALO - k:POOL_HALO - k + l, c0:c0 + grp]
        inv_cnt = 1.0 / jnp.minimum(pos + 1, w).astype(F32)
        diff = (acc * inv_cnt - hg).reshape(bt * l, grp).astype(BF16)
        y = _dot(diff, pw_ref[gi]) * sc_ref[:, c0:c0 + grp]
        o_ref[:, :, c0:c0 + grp] = x_ref[:, :, c0:c0 + grp] + y.reshape(bt, l, grp)


def _pool_prompt_kernel(x_ref, g_ref, pw_ref, sc_ref, o_ref, st_ref, xe_ref, carry_ref, *, pos0):
    i = pl.program_id(1)
    l = x_ref.shape[1]

    @pl.when(i == 0)
    def _():
        carry_ref[...] = jnp.zeros_like(carry_ref)

    _pool_body(x_ref, carry_ref[...], g_ref, pw_ref, sc_ref, o_ref, xe_ref, pos0 + i * l)
    carry_ref[...] = xe_ref[:, l:l + POOL_HALO, :]
    st_ref[...] = xe_ref[:, l + POOL_HALO - POOL_HIST:l + POOL_HALO, :]


def _pool_sample_kernel(x_ref, hist_ref, g_ref, pw_ref, sc_ref, o_ref, st_ref, xe_ref, *, pos0):
    l = x_ref.shape[1]
    _pool_body(x_ref, hist_ref[...], g_ref, pw_ref, sc_ref, o_ref, xe_ref, pos0)
    st_ref[...] = xe_ref[:, l + POOL_HALO - POOL_HIST:l + POOL_HALO, :]


def _pool_prompt(x, g, pw, scale, pos0):
    b, s, d = x.shape
    ts = TOKEN_TILE
    ng = len(POOL_WINDOWS)
    return pl.pallas_call(
        functools.partial(_pool_prompt_kernel, pos0=pos0),
        grid=(b, s // ts),
        in_specs=[
            pl.BlockSpec((1, ts, d), lambda bi, i: (bi, i, 0)),
            pl.BlockSpec((1, d), lambda bi, i: (0, 0)),
            pl.BlockSpec((ng, d // ng, d // ng), lambda bi, i: (0, 0, 0)),
            pl.BlockSpec((1, d), lambda bi, i: (0, 0)),
        ],
        out_specs=[
            pl.BlockSpec((1, ts, d), lambda bi, i: (bi, i, 0)),
            pl.BlockSpec((1, POOL_HIST, d), lambda bi, i: (bi, 0, 0)),
        ],
        out_shape=[jax.ShapeDtypeStruct((b, s, d), F32),
                   jax.ShapeDtypeStruct((b, POOL_HIST, d), F32)],
        scratch_shapes=[pltpu.VMEM((1, ts + POOL_HALO, d), F32),
                        pltpu.VMEM((1, POOL_HALO, d), F32)],
        compiler_params=_params(2),
        name="pool_prompt",
    )(x, g, pw, scale)


def _pool_sample(x, hist16, g, pw, scale, pos0):
    b, s, d = x.shape
    bt = 8
    ng = len(POOL_WINDOWS)
    return pl.pallas_call(
        functools.partial(_pool_sample_kernel, pos0=pos0),
        grid=(b // bt,),
        in_specs=[
            pl.BlockSpec((bt, s, d), lambda i: (i, 0, 0)),
            pl.BlockSpec((bt, POOL_HALO, d), lambda i: (i, 0, 0)),
            pl.BlockSpec((1, d), lambda i: (0, 0)),
            pl.BlockSpec((ng, d // ng, d // ng), lambda i: (0, 0, 0)),
            pl.BlockSpec((1, d), lambda i: (0, 0)),
        ],
        out_specs=[
            pl.BlockSpec((bt, s, d), lambda i: (i, 0, 0)),
            pl.BlockSpec((bt, POOL_HIST, d), lambda i: (i, 0, 0)),
        ],
        out_shape=[jax.ShapeDtypeStruct((b, s, d), F32),
                   jax.ShapeDtypeStruct((b, POOL_HIST, d), F32)],
        scratch_shapes=[pltpu.VMEM((bt, s + POOL_HALO, d), F32)],
        compiler_params=_params(1),
        name="pool_sample",
    )(x, hist16, g, pw, scale)


def _gmlp_kernel(x_ref, g_ref, win_ref, bin_ref, lng_ref, lnb_ref, ws_ref, bs_ref, wout_ref, o_ref, *rest,
                 block_diag, emit_v):
    if emit_v:
        vout_ref, z_ref, t_ref = rest
    else:
        z_ref, t_ref = rest
    tm = x_ref.shape[0]
    n_in, _, tn = win_ref.shape
    n_out, _, tno = wout_ref.shape
    width = n_in * tn // 2
    hdim = width // GMLP_HEADS

    h = _rms(x_ref[...], g_ref[...]).astype(BF16)
    for n in range(n_in):
        cols = slice(n * tn, (n + 1) * tn)
        z_ref[:, cols] = jax.nn.gelu(_dot(h, win_ref[n]) + bin_ref[:, cols])
    v = _layernorm(z_ref[:, width:], lng_ref[...], lnb_ref[...])
    if emit_v:
        vout_ref[...] = v
    v = v.astype(BF16)

    row = lax.broadcasted_iota(jnp.int32, (GMLP_CHUNK, GMLP_CHUNK), 0)
    col = lax.broadcasted_iota(jnp.int32, (GMLP_CHUNK, GMLP_CHUNK), 1)
    if block_diag:
        shift = block_diag.bit_length() - 1
        mask = (row >> shift) == (col >> shift)
    else:
        mask = (col >> GMLP_CAUSAL_SHIFT) <= (row >> GMLP_CAUSAL_SHIFT)
    for head in range(GMLP_HEADS):
        lanes = slice(head * hdim, (head + 1) * hdim)
        wm = jnp.where(mask, ws_ref[head], 0.0).astype(BF16)
        bias = bs_ref[head]
        for c in range(tm // GMLP_CHUNK):
            rows = slice(c * GMLP_CHUNK, (c + 1) * GMLP_CHUNK)
            s = _dot(wm, v[rows, lanes]) + bias
            t_ref[rows, lanes] = (z_ref[rows, lanes] * s).astype(BF16)
    t = t_ref[...]
    for n in range(n_out):
        cols = slice(n * tno, (n + 1) * tno)
        o_ref[:, cols] = x_ref[:, cols] + _dot(t, wout_ref[n])


def _gmlp(x, g, w_in, b_in, ln_g, ln_b, ws, bs, w_out, block_diag, emit_v):
    m, d = x.shape
    n_in, _, tn = w_in.shape
    width = n_in * tn // 2
    tm = GMLP_TOKEN_TILE

    def resident(shape):
        return pl.BlockSpec(shape, lambda i: (0,) * len(shape), pipeline_mode=pl.Buffered(1))

    out_specs = [pl.BlockSpec((tm, d), lambda i: (i, 0))]
    out_shape = [jax.ShapeDtypeStruct((m, d), F32)]
    if emit_v:
        out_specs.append(pl.BlockSpec((tm, width), lambda i: (i, 0)))
        out_shape.append(jax.ShapeDtypeStruct((m, width), F32))
    outs = pl.pallas_call(
        functools.partial(_gmlp_kernel, block_diag=block_diag, emit_v=emit_v),
        grid=(m // tm,),
        in_specs=[
            pl.BlockSpec((tm, d), lambda i: (i, 0)),
            resident((1, d)),
            resident(w_in.shape),
            resident((1, 2 * width)),
            resident((1, width)),
            resident((1, width)),
            resident((GMLP_HEADS, GMLP_CHUNK, GMLP_CHUNK)),
            resident((GMLP_HEADS, GMLP_CHUNK, 1)),
            resident(w_out.shape),
        ],
        out_specs=out_specs,
        out_shape=out_shape,
        scratch_shapes=[pltpu.VMEM((tm, 2 * width), F32),
                        pltpu.VMEM((tm, width), BF16)],
        compiler_params=_params(1),
        name="gmlp_v" if emit_v else "gmlp",
    )(x, g, w_in, b_in, ln_g, ln_b, ws, bs, w_out)
    return outs if emit_v else (outs[0], None)


def _merge_col_blocks(state):
    if state.ndim == 3:
        return state
    n_seq, nb, rows, tn = state.shape
    return state.transpose(0, 2, 1, 3).reshape(n_seq, rows, nb * tn)


def _sconv_kernel(x_ref, g_ref, win_ref, cw_ref, wout_ref, *rest, sample, tiles_per_seq):
    if sample:
        hist_ref, o_ref, st_ref, cxe_ref, gated_ref = rest
    else:
        o_ref, st_ref, cxe_ref, gated_ref, carry_ref = rest
    i = pl.program_id(0)
    tm = x_ref.shape[0]
    nb, _, tn = wout_ref.shape
    halo = SCONV_HALO
    first_tap = halo - (SCONV_WIDTH - 1)

    h = _rms(x_ref[...], g_ref[...]).astype(BF16)

    if not sample:
        @pl.when(i % tiles_per_seq == 0)
        def _():
            carry_ref[...] = jnp.zeros_like(carry_ref)

    for n in range(nb):
        cols = slice(n * tn, (n + 1) * tn)
        b_gate = _dot(h, win_ref[n])
        cx = _dot(h, win_ref[nb + n]) * _dot(h, win_ref[2 * nb + n])
        if sample:
            bt, l = cxe_ref.shape[0], cxe_ref.shape[1] - halo
            cxe_ref[:, 0:halo, :] = hist_ref[:, :, cols]
            cxe_ref[:, halo:, :] = cx.reshape(bt, l, tn)
            conv = None
            for k in range(SCONV_WIDTH):
                term = cw_ref[k:k + 1, cols] * cxe_ref[:, first_tap + k:first_tap + k + l, :]
                conv = term if conv is None else conv + term
            conv = conv.reshape(tm, tn)
            st_ref[:, :, cols] = cxe_ref[:, l + first_tap:l + halo, :]
        else:
            cxe_ref[n, 0:halo, :] = carry_ref[n]
            cxe_ref[n, halo:, :] = cx
            carry_ref[n] = cxe_ref[n, tm:tm + halo, :]
            conv = None
            for k in range(SCONV_WIDTH):
                term = cw_ref[k:k + 1, cols] * cxe_ref[n, first_tap + k:first_tap + k + tm, :]
                conv = term if conv is None else conv + term
            st_ref[i // tiles_per_seq, n] = cxe_ref[n, tm + first_tap:tm + halo, :]
        gated_ref[:, cols] = (b_gate * conv).astype(BF16)

    gated = gated_ref[...]
    for n in range(nb):
        cols = slice(n * tn, (n + 1) * tn)
        o_ref[:, cols] = x_ref[:, cols] + _dot(gated, wout_ref[n])


def _sconv(x, g, w_in, conv_w, w_out, hist8, n_seq):
    m, d = x.shape
    sample = hist8 is not None
    nb, _, tn = w_out.shape
    hist_rows = SCONV_WIDTH - 1
    tm = SCONV_TOKEN_TILE
    l = m // n_seq
    tiles_per_seq = None if sample else l // tm
    bt = tm // l if sample else None

    def resident(shape):
        return pl.BlockSpec(shape, lambda i: (0,) * len(shape), pipeline_mode=pl.Buffered(1))

    in_specs = [
        pl.BlockSpec((tm, d), lambda i: (i, 0)),
        resident((1, d)),
        resident(w_in.shape),
        resident((SCONV_WIDTH, d)),
        resident(w_out.shape),
    ]
    args = [x, g, w_in, conv_w, w_out]
    if sample:
        in_specs.append(pl.BlockSpec((bt, SCONV_HALO, d), lambda i: (i, 0, 0)))
        args.append(hist8)
        st_spec = pl.BlockSpec((bt, hist_rows, d), lambda i: (i, 0, 0))
        scratch = [pltpu.VMEM((bt, l + SCONV_HALO, tn), F32), pltpu.VMEM((tm, d), BF16)]
    else:
        st_spec = pl.BlockSpec((n_seq, nb, hist_rows, tn), lambda i: (0, 0, 0, 0))
        scratch = [pltpu.VMEM((nb, tm + SCONV_HALO, tn), F32), pltpu.VMEM((tm, d), BF16),
                   pltpu.VMEM((nb, SCONV_HALO, tn), F32)]
    st_shape = (n_seq, hist_rows, d) if sample else (n_seq, nb, hist_rows, tn)
    out, state = pl.pallas_call(
        functools.partial(_sconv_kernel, sample=sample, tiles_per_seq=tiles_per_seq),
        grid=(m // tm,),
        in_specs=in_specs,
        out_specs=[pl.BlockSpec((tm, d), lambda i: (i, 0)), st_spec],
        out_shape=[jax.ShapeDtypeStruct((m, d), F32),
                   jax.ShapeDtypeStruct(st_shape, F32)],
        scratch_shapes=scratch,
        compiler_params=_params(1),
        name="sconv_sample" if sample else "sconv_prompt",
    )(*args)
    return out, _merge_col_blocks(state)


def _causal_taps_flat(glue_ref, n, w_ref, c0, tn, tm, halo, width):
    first_tap = halo - (width - 1)
    ext = tm + SUBLANES
    conv = None
    for r in range(SUBLANES):
        part = None
        for o in range(first_tap, first_tap + width):
            if o % SUBLANES != r:
                continue
            k = o - first_tap
            rows = tm if r == 0 else ext
            term = w_ref[k:k + 1, c0:c0 + tn] * glue_ref[n, o - r:o - r + rows, :]
            part = term if part is None else part + term
        if part is None:
            continue
        if r:
            part = pltpu.roll(part, ext - r, axis=0)[:tm]
        conv = part if conv is None else conv + part
    return conv


def _cconv_kernel(x_ref, g_ref, w1_ref, b1_ref, dw_ref, dwb_ref, lng_ref, lnb_ref, w2_ref, b2_ref,
                  *rest, sample, tiles_per_seq, nb):
    if sample:
        hist_ref, o_ref, st_ref, glue_ref, conv_ref = rest
    else:
        o_ref, st_ref, glue_ref, conv_ref, carry_ref = rest
    i = pl.program_id(0)
    tm, d = x_ref.shape
    tn = d // nb
    halo = CCONV_HALO
    first_tap = halo - (CCONV_WIDTH - 1)

    x = x_ref[...]
    h = _rms(x, g_ref[...]).astype(BF16)

    if not sample:
        @pl.when(i % tiles_per_seq == 0)
        def _():
            carry_ref[...] = jnp.zeros_like(carry_ref)

    for n in range(nb):
        c0 = n * tn
        a = _dot(h, w1_ref[n]) + b1_ref[:, c0:c0 + tn]
        gt = _dot(h, w1_ref[nb + n]) + b1_ref[:, d + c0:d + c0 + tn]
        glu = a * jax.nn.sigmoid(gt)
        if sample:
            bt, l = glue_ref.shape[0], glue_ref.shape[1] - halo
            glue_ref[:, 0:halo, :] = hist_ref[:, :, c0:c0 + tn]
            glue_ref[:, halo:, :] = glu.reshape(bt, l, tn)
            conv = None
            for k in range(CCONV_WIDTH):
                term = dw_ref[k:k + 1, c0:c0 + tn] * glue_ref[:, first_tap + k:first_tap + k + l, :]
                conv = term if conv is None else conv + term
            conv = conv.reshape(tm, tn)
            st_ref[:, :, c0:c0 + tn] = glue_ref[:, l + first_tap:l + halo, :]
        else:
            glue_ref[n, 0:halo, :] = carry_ref[n]
            glue_ref[n, halo:, :] = glu
            carry_ref[n] = glue_ref[n, tm:tm + halo, :]
            conv = _causal_taps_flat(glue_ref, n, dw_ref, c0, tn, tm, halo, CCONV_WIDTH)
            st_ref[i // tiles_per_seq, n] = glue_ref[n, tm + first_tap:tm + halo, :]
        conv_ref[:, c0:c0 + tn] = conv + dwb_ref[:, c0:c0 + tn]

    y = _layernorm(conv_ref[...], lng_ref[...], lnb_ref[...])
    z = (y * jax.nn.sigmoid(y)).astype(BF16)
    for n in range(nb):
        cols = slice(n * tn, (n + 1) * tn)
        o_ref[:, cols] = x_ref[:, cols] + _dot(z, w2_ref[n]) + b2_ref[:, cols]


def _cast_kernel(w_ref, o_ref):
    o_ref[...] = w_ref[...].astype(o_ref.dtype)


def _col_blocks_bf16(w, tn):
    k, n = w.shape
    return pl.pallas_call(
        _cast_kernel,
        grid=(n // tn,),
        in_specs=[pl.BlockSpec((k, tn), lambda j: (0, j))],
        out_specs=pl.BlockSpec((None, k, tn), lambda j: (j, 0, 0)),
        out_shape=jax.ShapeDtypeStruct((n // tn, k, tn), BF16),
        compiler_params=_params(1),
        name="cast_col_blocks",
    )(w)


def _cconv(x, g, w1, b1, dw_w, dw_b, ln_g, ln_b, w2, b2, hist32, n_seq):
    m, d = x.shape
    sample = hist32 is not None
    nb, _, tn = w2.shape
    hist_rows = CCONV_WIDTH - 1
    tm = CCONV_TOKEN_TILE
    l = m // n_seq
    tiles_per_seq = None if sample else l // tm
    bt = tm // l if sample else None

    def resident(shape):
        return pl.BlockSpec(shape, lambda i: (0,) * len(shape), pipeline_mode=pl.Buffered(1))

    in_specs = [
        pl.BlockSpec((tm, d), lambda i: (i, 0)),
        resident((1, d)),
        resident((2 * nb, d, tn)),
        resident((1, 2 * d)),
        resident((CCONV_WIDTH, d)),
        resident((1, d)),
        resident((1, d)),
        resident((1, d)),
        resident((nb, d, tn)),
        resident((1, d)),
    ]
    args = [x, g, w1, b1, dw_w, dw_b, ln_g, ln_b, w2, b2]
    if sample:
        in_specs.append(pl.BlockSpec((bt, CCONV_HALO, d), lambda i: (i, 0, 0), pipeline_mode=pl.Buffered(1)))
        args.append(hist32)
        st_spec = pl.BlockSpec((bt, hist_rows, d), lambda i: (i, 0, 0))
        scratch = [pltpu.VMEM((bt, l + CCONV_HALO, tn), F32), pltpu.VMEM((tm, d), F32)]
    else:
        st_spec = pl.BlockSpec((n_seq, nb, hist_rows, tn), lambda i: (0, 0, 0, 0))
        scratch = [pltpu.VMEM((nb, tm + CCONV_HALO, tn), F32), pltpu.VMEM((tm, d), F32),
                   pltpu.VMEM((nb, CCONV_HALO, tn), F32)]
    st_shape = (n_seq, hist_rows, d) if sample else (n_seq, nb, hist_rows, tn)
    out, state = pl.pallas_call(
        functools.partial(_cconv_kernel, sample=sample, tiles_per_seq=tiles_per_seq, nb=nb),
        grid=(m // tm,),
        in_specs=in_specs,
        out_specs=[pl.BlockSpec((tm, d), lambda i: (i, 0)), st_spec],
        out_shape=[jax.ShapeDtypeStruct((m, d), F32),
                   jax.ShapeDtypeStruct(st_shape, F32)],
        scratch_shapes=scratch,
        compiler_params=_params(1),
        name="cconv_sample" if sample else "cconv_prompt",
    )(*args)
    return out, _merge_col_blocks(state)


def _pad_front(hist, rows):
    return jnp.pad(hist, ((0, 0), (rows - hist.shape[1], 0), (0, 0)))


def _trunk(x, pool_hist, sconv_hist, cconv_hist, pos0, p, ffn_weights):
    b, s, d = x.shape
    sample = pool_hist is not None
    row = lambda v: v.reshape(1, -1)

    if sample:
        x, pool_state = _pool_sample(x, _pad_front(pool_hist, POOL_HALO), row(p['norm_mix_g'][0]),
                                     p['pool_w'], row(p['pool_scale']), pos0)
    else:
        x, pool_state = _pool_prompt(x, row(p['norm_mix_g'][0]), p['pool_w'], row(p['pool_scale']), pos0)
    x = x.reshape(b * s, d)
    used_weights = []

    def ffn(x, layer, g_final=None):
        if ffn_weights is None:
            w = (p['ffn_w_gate'], p['ffn_w_up'], p['ffn_w_down'])
        else:
            w = ffn_weights[layer]
        out, w_bf16 = _ffn(x, row(p['norm_ffn_g'][layer]), *w, layer, g_final)
        used_weights.append(w_bf16 or w)
        return out

    x = ffn(x, 0)

    if sample:
        reps = GMLP_CHUNK // s
        ws = jnp.tile(p['gmlp_w_s'][:, :s, :s], (1, reps, reps))
        bs = jnp.tile(p['gmlp_b_s'][:, :s], (1, reps))
        block_diag = s
    else:
        ws, bs, block_diag = p['gmlp_w_s'], p['gmlp_b_s'], 0
    x, v = _gmlp(x, row(p['norm_mix_g'][1]), p['gmlp_w_in'], row(p['gmlp_b_in']), row(p['gmlp_ln_g']),
                 row(p['gmlp_ln_b']), ws, bs[:, :, None], p['gmlp_w_out'], block_diag, emit_v=sample)
    x = ffn(x, 1)

    hist8 = _pad_front(sconv_hist, SCONV_HALO) if sample else None
    x, sconv_state = _sconv(x, row(p['norm_mix_g'][2]), p['sconv_w_in'], p['sconv_conv_w'],
                            p['sconv_w_out'], hist8, b)
    x = ffn(x, 2)

    hist32 = _pad_front(cconv_hist, CCONV_HALO) if sample else None
    x, cconv_state = _cconv(x, row(p['norm_mix_g'][3]), p['cconv_w_pw1'], row(p['cconv_b_pw1']),
                            p['cconv_dw_w'], row(p['cconv_dw_b']), row(p['cconv_ln_g']),
                            row(p['cconv_ln_b']), p['cconv_w_pw2'], row(p['cconv_b_pw2']), hist32, b)
    y = ffn(x, 3, row(p['norm_final_g']))
    gmlp_v = v.reshape(b, s, -1) if sample else None
    return y.reshape(b, s, d), pool_state, gmlp_v, sconv_state, cconv_state, used_weights


MIXER_MATMUL_WEIGHTS = ('pool_w',)
COL_BLOCKED_WEIGHTS = ('gmlp_w_in', 'gmlp_w_out', 'sconv_w_in', 'sconv_w_out', 'cconv_w_pw1', 'cconv_w_pw2')


def kernel(x_prompt, x_sample, state_pool, state_sconv, state_cconv, norm_mix_g, norm_ffn_g, norm_final_g, pool_w, pool_scale, gmlp_w_in, gmlp_b_in, gmlp_ln_g, gmlp_ln_b, gmlp_w_s, gmlp_b_s, gmlp_w_out, sconv_w_in, sconv_conv_w, sconv_w_out, cconv_w_pw1, cconv_b_pw1, cconv_dw_w, cconv_dw_b, cconv_ln_g, cconv_ln_b, cconv_w_pw2, cconv_b_pw2, ffn_w_gate, ffn_w_up, ffn_w_down):
    p = dict(norm_mix_g=norm_mix_g, norm_ffn_g=norm_ffn_g, norm_final_g=norm_final_g,
             pool_w=pool_w, pool_scale=pool_scale,
             gmlp_w_in=gmlp_w_in, gmlp_b_in=gmlp_b_in, gmlp_ln_g=gmlp_ln_g, gmlp_ln_b=gmlp_ln_b,
             gmlp_w_s=gmlp_w_s, gmlp_b_s=gmlp_b_s, gmlp_w_out=gmlp_w_out,
             sconv_w_in=sconv_w_in, sconv_conv_w=sconv_conv_w, sconv_w_out=sconv_w_out,
             cconv_w_pw1=cconv_w_pw1, cconv_b_pw1=cconv_b_pw1, cconv_dw_w=cconv_dw_w, cconv_dw_b=cconv_dw_b,
             cconv_ln_g=cconv_ln_g, cconv_ln_b=cconv_ln_b, cconv_w_pw2=cconv_w_pw2, cconv_b_pw2=cconv_b_pw2,
             ffn_w_gate=ffn_w_gate, ffn_w_up=ffn_w_up, ffn_w_down=ffn_w_down)
    for name in MIXER_MATMUL_WEIGHTS:
        p[name] = p[name].astype(BF16)
    for name in COL_BLOCKED_WEIGHTS:
        p[name] = _col_blocks_bf16(p[name], COL_TILE)
    y_s, pool_s, gmlp_v_s, sconv_s, cconv_s, ffn_weights = _trunk(
        x_sample, state_pool, state_sconv, state_cconv, PAST_LEN, p, None)
    y_p, pool_p, _, sconv_p, cconv_p, _ = _trunk(x_prompt, None, None, None, 0, p, ffn_weights)
    return (y_p, y_s, pool_p, pool_s, gmlp_v_s, sconv_p, sconv_s, cconv_p, cconv_s)
```

```python
import functools

import jax
import jax.numpy as jnp
from jax import lax
from jax.experimental import pallas as pl
from jax.experimental.pallas import tpu as pltpu

F32 = jnp.float32
BF16 = jnp.bfloat16

EPS = 1e-6
PAST_LEN = 4096
POOL_WINDOWS = (2, 4, 8, 16)
POOL_HIST = max(POOL_WINDOWS) - 1
POOL_HALO = 16
GMLP_CHUNK = 128
GMLP_CAUSAL_SHIFT = 6
GMLP_HEADS = 8
GMLP_TOKEN_TILE = 256
SCONV_TOKEN_TILE = 256
SCONV_WIDTH = 3
SCONV_HALO = 8
CCONV_WIDTH = 31
CCONV_HALO = 32

SUBLANES = 8
TOKEN_TILE = 512
CCONV_TOKEN_TILE = 256
FFN_TOKEN_TILE = 1024
FFN_CAST_TOKEN_TILE = 512
FFN_CAST_COL_TILE = 256
COL_TILE = 512
VMEM_LIMIT_BYTES = 56 * 1024 * 1024


def _rms(x, g):
    ms = jnp.mean(x * x, axis=-1, keepdims=True)
    return x * lax.rsqrt(ms + EPS) * g


def _layernorm(x, g, b):
    mu = jnp.mean(x, axis=-1, keepdims=True)
    xc = x - mu
    var = jnp.mean(xc * xc, axis=-1, keepdims=True)
    return xc * lax.rsqrt(var + EPS) * g + b


def _dot(a, b):
    return jnp.dot(a, b, preferred_element_type=F32)


def _params(n_axes):
    return pltpu.CompilerParams(
        dimension_semantics=("arbitrary",) * n_axes,
        vmem_limit_bytes=VMEM_LIMIT_BYTES)


def _ffn_kernel(x_ref, g_ref, wg_ref, wu_ref, wd_ref, *rest, final_norm):
    rest = list(rest)
    gf_ref = rest.pop(0) if final_norm else None
    o_ref, wg_out_ref, wu_out_ref, wd_out_ref, h_ref = rest
    k = pl.program_id(1)

    @pl.when(k == 0)
    def _():
        x = x_ref[...]
        h_ref[...] = _rms(x, g_ref[...]).astype(BF16)
        o_ref[...] = x

    wg, wu, wd = wg_ref[...].astype(BF16), wu_ref[...].astype(BF16), wd_ref[...].astype(BF16)
    wg_out_ref[...] = wg
    wu_out_ref[...] = wu
    wd_out_ref[...] = wd
    h = h_ref[...]
    gate = _dot(h, wg)
    up = _dot(h, wu)
    act = (gate * jax.nn.sigmoid(gate) * up).astype(BF16)
    o_ref[...] += _dot(act, wd)

    if final_norm:
        @pl.when(k == pl.num_programs(1) - 1)
        def _():
            o_ref[...] = _rms(o_ref[...], gf_ref[...])


def _ffn_loop_kernel(x_hbm, g_ref, wg_hbm, wu_hbm, wd_hbm, *rest, final_norm):
    rest = list(rest)
    gf_ref = rest.pop(0) if final_norm else None
    o_ref, h_ref, x_tile, wg_buf, wu_buf, wd_buf, x_sem, w_sem = rest
    i = pl.program_id(0)
    n_tiles = pl.num_programs(0)
    tm = x_tile.shape[0]
    tf = wg_buf.shape[2]
    nk = wg_hbm.shape[1] // tf

    def fetch_x(tile):
        return pltpu.make_async_copy(x_hbm.at[pl.ds(tile * tm, tm)], x_tile, x_sem)

    def fetch_w(k, slot):
        cols = pl.ds(pl.multiple_of(k * tf, tf), tf)
        return (pltpu.make_async_copy(wg_hbm.at[:, cols], wg_buf.at[slot], w_sem.at[0, slot]),
                pltpu.make_async_copy(wu_hbm.at[:, cols], wu_buf.at[slot], w_sem.at[1, slot]),
                pltpu.make_async_copy(wd_hbm.at[cols, :], wd_buf.at[slot], w_sem.at[2, slot]))

    first_slot = (i * nk) % 2

    @pl.when(i == 0)
    def _():
        fetch_x(0).start()
        for c in fetch_w(0, 0):
            c.start()

    fetch_x(i).wait()
    x = x_tile[...]
    h_ref[...] = _rms(x, g_ref[...]).astype(BF16)
    o_ref[...] = x

    def step(k, carry):
        slot = (first_slot + k) % 2
        for c in fetch_w(k, slot):
            c.wait()
        more_here = k + 1 < nk

        @pl.when(more_here | (i + 1 < n_tiles))
        def _():
            for c in fetch_w(jnp.where(more_here, k + 1, 0), 1 - slot):
                c.start()

        @pl.when((k == 1) & (i + 1 < n_tiles))
        def _():
            fetch_x(i + 1).start()

        h = h_ref[...]
        gate = _dot(h, wg_buf[slot])
        up = _dot(h, wu_buf[slot])
        act = (gate * jax.nn.sigmoid(gate) * up).astype(BF16)
        o_ref[...] += _dot(act, wd_buf[slot])
        return carry

    lax.fori_loop(0, nk, step, 0)

    if final_norm:
        o_ref[...] = _rms(o_ref[...], gf_ref[...])


def _ffn(x, g, wg, wu, wd, layer, g_final=None):
    m, d = x.shape
    emit_bf16 = wg.ndim == 3
    f = wg.shape[-1]
    if emit_bf16:
        tm, tf = FFN_CAST_TOKEN_TILE, FFN_CAST_COL_TILE
        assert m == tm, "every weight block must be visited exactly once"
        w_specs = [
            pl.BlockSpec((None, d, tf), lambda i, k: (layer, 0, k)),
            pl.BlockSpec((None, d, tf), lambda i, k: (layer, 0, k)),
            pl.BlockSpec((None, tf, d), lambda i, k: (layer, k, 0)),
        ]
    else:
        tm, tf = FFN_TOKEN_TILE, COL_TILE
    final_norm = g_final is not None
    if not emit_bf16:
        any_spec = pl.BlockSpec(memory_space=pl.ANY)
        in_specs = [any_spec, pl.BlockSpec((1, d), lambda i: (0, 0)), any_spec, any_spec, any_spec]
        args = [x, g, wg, wu, wd]
        if final_norm:
            in_specs.append(pl.BlockSpec((1, d), lambda i: (0, 0)))
            args.append(g_final)
        out = pl.pallas_call(
            functools.partial(_ffn_loop_kernel, final_norm=final_norm),
            grid=(m // tm,),
            in_specs=in_specs,
            out_specs=pl.BlockSpec((tm, d), lambda i: (i, 0)),
            out_shape=jax.ShapeDtypeStruct((m, d), F32),
            scratch_shapes=[pltpu.VMEM((tm, d), BF16), pltpu.VMEM((tm, d), F32),
                            pltpu.VMEM((2, d, tf), BF16), pltpu.VMEM((2, d, tf), BF16),
                            pltpu.VMEM((2, tf, d), BF16),
                            pltpu.SemaphoreType.DMA(()), pltpu.SemaphoreType.DMA((3, 2))],
            compiler_params=_params(1),
            name="ffn_final" if final_norm else "ffn",
        )(*args)
        return out, ()
    in_specs = [pl.BlockSpec((tm, d), lambda i, k: (i, 0)),
                pl.BlockSpec((1, d), lambda i, k: (0, 0))] + w_specs
    args = [x, g, wg, wu, wd]
    if final_norm:
        in_specs.append(pl.BlockSpec((1, d), lambda i, k: (0, 0)))
        args.append(g_final)
    out_specs = [pl.BlockSpec((tm, d), lambda i, k: (i, 0)),
                 pl.BlockSpec((d, tf), lambda i, k: (0, k)),
                 pl.BlockSpec((d, tf), lambda i, k: (0, k)),
                 pl.BlockSpec((tf, d), lambda i, k: (k, 0))]
    out_shape = [jax.ShapeDtypeStruct((m, d), F32), jax.ShapeDtypeStruct((d, f), BF16),
                 jax.ShapeDtypeStruct((d, f), BF16), jax.ShapeDtypeStruct((f, d), BF16)]
    outs = pl.pallas_call(
        functools.partial(_ffn_kernel, final_norm=final_norm),
        grid=(m // tm, f // tf),
        in_specs=in_specs,
        out_specs=out_specs,
        out_shape=out_shape,
        scratch_shapes=[pltpu.VMEM((tm, d), BF16)],
        compiler_params=_params(2),
        name="ffn_cast" + ("_final" if final_norm else ""),
    )(*args)
    return outs[0], tuple(outs[1:])


def _pool_body(x_ref, hist, g_ref, pw_ref, sc_ref, o_ref, xe_ref, pos_first):
    bt, l, d = x_ref.shape
    grp = d // len(POOL_WINDOWS)
    x = x_ref[...]
    xe_ref[:, 0:POOL_HALO, :] = hist
    xe_ref[:, POOL_HALO:, :] = _rms(x, g_ref[...])
    pos = lax.broadcasted_iota(jnp.int32, (1, l, 1), 1) + pos_first
    for gi, w in enumerate(POOL_WINDOWS):
        c0 = gi * grp
        hg = xe_ref[:, POOL_HALO:, c0:c0 + grp]
        if bt == 1:
            a = xe_ref[0, :, c0:c0 + grp]
            span = 1
            while span < min(w, SUBLANES):
                a = a + pltpu.roll(a, span, axis=0)
                span *= 2
            acc = a[POOL_HALO:POOL_HALO + l]
            if w > SUBLANES:
                assert w == 2 * SUBLANES
                acc = acc + a[POOL_HALO - SUBLANES:POOL_HALO - SUBLANES + l]
            acc = acc[None]
        else:
            acc = hg
            for k in range(1, w):
                acc = acc + xe_ref[:, POOL_HALO - k:POOL_HALO - k + l, c0:c0 + grp]
        inv_cnt = 1.0 / jnp.minimum(pos + 1, w).astype(F32)
        diff = (acc * inv_cnt - hg).reshape(bt * l, grp).astype(BF16)
        y = _dot(diff, pw_ref[gi]) * sc_ref[:, c0:c0 + grp]
        o_ref[:, :, c0:c0 + grp] = x_ref[:, :, c0:c0 + grp] + y.reshape(bt, l, grp)


def _pool_prompt_kernel(x_ref, g_ref, pw_ref, sc_ref, o_ref, st_ref, xe_ref, carry_ref, *, pos0):
    i = pl.program_id(1)
    l = x_ref.shape[1]

    @pl.when(i == 0)
    def _():
        carry_ref[...] = jnp.zeros_like(carry_ref)

    _pool_body(x_ref, carry_ref[...], g_ref, pw_ref, sc_ref, o_ref, xe_ref, pos0 + i * l)
    carry_ref[...] = xe_ref[:, l:l + POOL_HALO, :]
    st_ref[...] = xe_ref[:, l + POOL_HALO - POOL_HIST:l + POOL_HALO, :]


def _pool_sample_kernel(x_ref, hist_ref, g_ref, pw_ref, sc_ref, o_ref, st_ref, xe_ref, *, pos0):
    l = x_ref.shape[1]
    _pool_body(x_ref, hist_ref[...], g_ref, pw_ref, sc_ref, o_ref, xe_ref, pos0)
    st_ref[...] = xe_ref[:, l + POOL_HALO - POOL_HIST:l + POOL_HALO, :]


def _pool_prompt(x, g, pw, scale, pos0):
    b, s, d = x.shape
    ts = TOKEN_TILE
    ng = len(POOL_WINDOWS)
    return pl.pallas_call(
        functools.partial(_pool_prompt_kernel, pos0=pos0),
        grid=(b, s // ts),
        in_specs=[
            pl.BlockSpec((1, ts, d), lambda bi, i: (bi, i, 0)),
            pl.BlockSpec((1, d), lambda bi, i: (0, 0)),
            pl.BlockSpec((ng, d // ng, d // ng), lambda bi, i: (0, 0, 0)),
            pl.BlockSpec((1, d), lambda bi, i: (0, 0)),
        ],
        out_specs=[
            pl.BlockSpec((1, ts, d), lambda bi, i: (bi, i, 0)),
            pl.BlockSpec((1, POOL_HIST, d), lambda bi, i: (bi, 0, 0)),
        ],
        out_shape=[jax.ShapeDtypeStruct((b, s, d), F32),
                   jax.ShapeDtypeStruct((b, POOL_HIST, d), F32)],
        scratch_shapes=[pltpu.VMEM((1, ts + POOL_HALO, d), F32),
                        pltpu.VMEM((1, POOL_HALO, d), F32)],
        compiler_params=_params(2),
        name="pool_prompt",
    )(x, g, pw, scale)


def _pool_sample(x, hist16, g, pw, scale, pos0):
    b, s, d = x.shape
    bt = 8
    ng = len(POOL_WINDOWS)
    return pl.pallas_call(
        functools.partial(_pool_sample_kernel, pos0=pos0),
        grid=(b // bt,),
        in_specs=[
            pl.BlockSpec((bt, s, d), lambda i: (i, 0, 0)),
            pl.BlockSpec((bt, POOL_HALO, d), lambda i: (i, 0, 0)),
            pl.BlockSpec((1, d), lambda i: (0, 0)),
            pl.BlockSpec((ng, d // ng, d // ng), lambda i: (0, 0, 0)),
            pl.BlockSpec((1, d), lambda i: (0, 0)),
        ],
        out_specs=[
            pl.BlockSpec((bt, s, d), lambda i: (i, 0, 0)),
            pl.BlockSpec((bt, POOL_HIST, d), lambda i: (i, 0, 0)),
        ],
        out_shape=[jax.ShapeDtypeStruct((b, s, d), F32),
                   jax.ShapeDtypeStruct((b, POOL_HIST, d), F32)],
        scratch_shapes=[pltpu.VMEM((bt, s + POOL_HALO, d), F32)],
        compiler_params=_params(1),
        name="pool_sample",
    )(x, hist16, g, pw, scale)


def _gmlp_kernel(x_ref, g_ref, win_ref, bin_ref, lng_ref, lnb_ref, ws_ref, bs_ref, wout_ref, o_ref, *rest,
                 block_diag, emit_v):
    if emit_v:
        vout_ref, z_ref, t_ref = rest
    else:
        z_ref, t_ref = rest
    tm = x_ref.shape[0]
    n_in, _, tn = win_ref.shape
    n_out, _, tno = wout_ref.shape
    width = n_in * tn // 2
    hdim = width // GMLP_HEADS

    h = _rms(x_ref[...], g_ref[...]).astype(BF16)
    for n in range(n_in):
        cols = slice(n * tn, (n + 1) * tn)
        z_ref[:, cols] = jax.nn.gelu(_dot(h, win_ref[n]) + bin_ref[:, cols])
    v = _layernorm(z_ref[:, width:], lng_ref[...], lnb_ref[...])
    if emit_v:
        vout_ref[...] = v
    v = v.astype(BF16)

    row = lax.broadcasted_iota(jnp.int32, (GMLP_CHUNK, GMLP_CHUNK), 0)
    col = lax.broadcasted_iota(jnp.int32, (GMLP_CHUNK, GMLP_CHUNK), 1)
    if block_diag:
        shift = block_diag.bit_length() - 1
        mask = (row >> shift) == (col >> shift)
    else:
        mask = (col >> GMLP_CAUSAL_SHIFT) <= (row >> GMLP_CAUSAL_SHIFT)
    for head in range(GMLP_HEADS):
        lanes = slice(head * hdim, (head + 1) * hdim)
        wm = jnp.where(mask, ws_ref[head], 0.0).astype(BF16)
        bias = bs_ref[head]
        for c in range(tm // GMLP_CHUNK):
            rows = slice(c * GMLP_CHUNK, (c + 1) * GMLP_CHUNK)
            s = _dot(wm, v[rows, lanes]) + bias
            t_ref[rows, lanes] = (z_ref[rows, lanes] * s).astype(BF16)
    t = t_ref[...]
    for n in range(n_out):
        cols = slice(n * tno, (n + 1) * tno)
        o_ref[:, cols] = x_ref[:, cols] + _dot(t, wout_ref[n])


def _gmlp(x, g, w_in, b_in, ln_g, ln_b, ws, bs, w_out, block_diag, emit_v):
    m, d = x.shape
    n_in, _, tn = w_in.shape
    width = n_in * tn // 2
    tm = GMLP_TOKEN_TILE

    def resident(shape):
        return pl.BlockSpec(shape, lambda i: (0,) * len(shape), pipeline_mode=pl.Buffered(1))

    out_specs = [pl.BlockSpec((tm, d), lambda i: (i, 0))]
    out_shape = [jax.ShapeDtypeStruct((m, d), F32)]
    if emit_v:
        out_specs.append(pl.BlockSpec((tm, width), lambda i: (i, 0)))
        out_shape.append(jax.ShapeDtypeStruct((m, width), F32))
    outs = pl.pallas_call(
        functools.partial(_gmlp_kernel, block_diag=block_diag, emit_v=emit_v),
        grid=(m // tm,),
        in_specs=[
            pl.BlockSpec((tm, d), lambda i: (i, 0)),
            resident((1, d)),
            resident(w_in.shape),
            resident((1, 2 * width)),
            resident((1, width)),
            resident((1, width)),
            resident((GMLP_HEADS, GMLP_CHUNK, GMLP_CHUNK)),
            resident((GMLP_HEADS, GMLP_CHUNK, 1)),
            resident(w_out.shape),
        ],
        out_specs=out_specs,
        out_shape=out_shape,
        scratch_shapes=[pltpu.VMEM((tm, 2 * width), F32),
                        pltpu.VMEM((tm, width), BF16)],
        compiler_params=_params(1),
        name="gmlp_v" if emit_v else "gmlp",
    )(x, g, w_in, b_in, ln_g, ln_b, ws, bs, w_out)
    return outs if emit_v else (outs[0], None)


def _merge_col_blocks(state):
    if state.ndim == 3:
        return state
    n_seq, nb, rows, tn = state.shape
    return state.transpose(0, 2, 1, 3).reshape(n_seq, rows, nb * tn)


def _sconv_kernel(x_ref, g_ref, win_ref, cw_ref, wout_ref, *rest, sample, tiles_per_seq):
    if sample:
        hist_ref, o_ref, st_ref, cxe_ref, gated_ref = rest
    else:
        o_ref, st_ref, cxe_ref, gated_ref, carry_ref = rest
    i = pl.program_id(0)
    tm = x_ref.shape[0]
    nb, _, tn = wout_ref.shape
    halo = SCONV_HALO
    first_tap = halo - (SCONV_WIDTH - 1)

    h = _rms(x_ref[...], g_ref[...]).astype(BF16)

    if not sample:
        @pl.when(i % tiles_per_seq == 0)
        def _():
            carry_ref[...] = jnp.zeros_like(carry_ref)

    for n in range(nb):
        cols = slice(n * tn, (n + 1) * tn)
        b_gate = _dot(h, win_ref[n])
        cx = _dot(h, win_ref[nb + n]) * _dot(h, win_ref[2 * nb + n])
        if sample:
            bt, l = cxe_ref.shape[0], cxe_ref.shape[1] - halo
            cxe_ref[:, 0:halo, :] = hist_ref[:, :, cols]
            cxe_ref[:, halo:, :] = cx.reshape(bt, l, tn)
            conv = None
            for k in range(SCONV_WIDTH):
                term = cw_ref[k:k + 1, cols] * cxe_ref[:, first_tap + k:first_tap + k + l, :]
                conv = term if conv is None else conv + term
            conv = conv.reshape(tm, tn)
            st_ref[:, :, cols] = cxe_ref[:, l + first_tap:l + halo, :]
        else:
            cxe_ref[n, 0:halo, :] = carry_ref[n]
            cxe_ref[n, halo:, :] = cx
            carry_ref[n] = cxe_ref[n, tm:tm + halo, :]
            conv = None
            for k in range(SCONV_WIDTH):
                term = cw_ref[k:k + 1, cols] * cxe_ref[n, first_tap + k:first_tap + k + tm, :]
                conv = term if conv is None else conv + term
            st_ref[i // tiles_per_seq, n] = cxe_ref[n, tm + first_tap:tm + halo, :]
        gated_ref[:, cols] = (b_gate * conv).astype(BF16)

    gated = gated_ref[...]
    for n in range(nb):
        cols = slice(n * tn, (n + 1) * tn)
        o_ref[:, cols] = x_ref[:, cols] + _dot(gated, wout_ref[n])


def _sconv(x, g, w_in, conv_w, w_out, hist8, n_seq):
    m, d = x.shape
    sample = hist8 is not None
    nb, _, tn = w_out.shape
    hist_rows = SCONV_WIDTH - 1
    tm = SCONV_TOKEN_TILE
    l = m // n_seq
    tiles_per_seq = None if sample else l // tm
    bt = tm // l if sample else None

    def resident(shape):
        return pl.BlockSpec(shape, lambda i: (0,) * len(shape), pipeline_mode=pl.Buffered(1))

    in_specs = [
        pl.BlockSpec((tm, d), lambda i: (i, 0)),
        resident((1, d)),
        resident(w_in.shape),
        resident((SCONV_WIDTH, d)),
        resident(w_out.shape),
    ]
    args = [x, g, w_in, conv_w, w_out]
    if sample:
        in_specs.append(pl.BlockSpec((bt, SCONV_HALO, d), lambda i: (i, 0, 0)))
        args.append(hist8)
        st_spec = pl.BlockSpec((bt, hist_rows, d), lambda i: (i, 0, 0))
        scratch = [pltpu.VMEM((bt, l + SCONV_HALO, tn), F32), pltpu.VMEM((tm, d), BF16)]
    else:
        st_spec = pl.BlockSpec((n_seq, nb, hist_rows, tn), lambda i: (0, 0, 0, 0))
        scratch = [pltpu.VMEM((nb, tm + SCONV_HALO, tn), F32), pltpu.VMEM((tm, d), BF16),
                   pltpu.VMEM((nb, SCONV_HALO, tn), F32)]
    st_shape = (n_seq, hist_rows, d) if sample else (n_seq, nb, hist_rows, tn)
    out, state = pl.pallas_call(
        functools.partial(_sconv_kernel, sample=sample, tiles_per_seq=tiles_per_seq),
        grid=(m // tm,),
        in_specs=in_specs,
        out_specs=[pl.BlockSpec((tm, d), lambda i: (i, 0)), st_spec],
        out_shape=[jax.ShapeDtypeStruct((m, d), F32),
                   jax.ShapeDtypeStruct(st_shape, F32)],
        scratch_shapes=scratch,
        compiler_params=_params(1),
        name="sconv_sample" if sample else "sconv_prompt",
    )(*args)
    return out, _merge_col_blocks(state)


def _causal_taps_flat(glue_ref, n, w_ref, c0, tn, tm, halo, width):
    first_tap = halo - (width - 1)
    ext = tm + SUBLANES
    conv = None
    for r in range(SUBLANES):
        part = None
        for o in range(first_tap, first_tap + width):
            if o % SUBLANES != r:
                continue
            k = o - first_tap
            rows = tm if r == 0 else ext
            term = w_ref[k:k + 1, c0:c0 + tn] * glue_ref[n, o - r:o - r + rows, :]
            part = term if part is None else part + term
        if part is None:
            continue
        if r:
            part = pltpu.roll(part, ext - r, axis=0)[:tm]
        conv = part if conv is None else conv + part
    return conv


def _cconv_kernel(x_ref, g_ref, w1_ref, b1_ref, dw_ref, dwb_ref, lng_ref, lnb_ref, w2_ref, b2_ref,
                  *rest, sample, tiles_per_seq, nb):
    if sample:
        hist_ref, o_ref, st_ref, glue_ref, conv_ref = rest
    else:
        o_ref, st_ref, glue_ref, conv_ref, carry_ref = rest
    i = pl.program_id(0)
    tm, d = x_ref.shape
    tn = d // nb
    halo = CCONV_HALO
    first_tap = halo - (CCONV_WIDTH - 1)

    x = x_ref[...]
    h = _rms(x, g_ref[...]).astype(BF16)

    if not sample:
        @pl.when(i % tiles_per_seq == 0)
        def _():
            carry_ref[...] = jnp.zeros_like(carry_ref)

    for n in range(nb):
        c0 = n * tn
        a = _dot(h, w1_ref[n]) + b1_ref[:, c0:c0 + tn]
        gt = _dot(h, w1_ref[nb + n]) + b1_ref[:, d + c0:d + c0 + tn]
        glu = a * jax.nn.sigmoid(gt)
        if sample:
            bt, l = glue_ref.shape[0], glue_ref.shape[1] - halo
            glue_ref[:, 0:halo, :] = hist_ref[:, :, c0:c0 + tn]
            glue_ref[:, halo:, :] = glu.reshape(bt, l, tn)
            conv = None
            for k in range(CCONV_WIDTH):
                term = dw_ref[k:k + 1, c0:c0 + tn] * glue_ref[:, first_tap + k:first_tap + k + l, :]
                conv = term if conv is None else conv + term
            conv = conv.reshape(tm, tn)
            st_ref[:, :, c0:c0 + tn] = glue_ref[:, l + first_tap:l + halo, :]
        else:
            glue_ref[n, 0:halo, :] = carry_ref[n]
            glue_ref[n, halo:, :] = glu
            carry_ref[n] = glue_ref[n, tm:tm + halo, :]
            conv = _causal_taps_flat(glue_ref, n, dw_ref, c0, tn, tm, halo, CCONV_WIDTH)
            st_ref[i // tiles_per_seq, n] = glue_ref[n, tm + first_tap:tm + halo, :]
        conv_ref[:, c0:c0 + tn] = conv + dwb_ref[:, c0:c0 + tn]

    y = _layernorm(conv_ref[...], lng_ref[...], lnb_ref[...])
    z = (y * jax.nn.sigmoid(y)).astype(BF16)
    for n in range(nb):
        cols = slice(n * tn, (n + 1) * tn)
        o_ref[:, cols] = x_ref[:, cols] + _dot(z, w2_ref[n]) + b2_ref[:, cols]


def _cast_kernel(w_ref, o_ref):
    o_ref[...] = w_ref[...].astype(o_ref.dtype)


def _col_blocks_bf16(w, tn):
    k, n = w.shape
    return pl.pallas_call(
        _cast_kernel,
        grid=(n // tn,),
        in_specs=[pl.BlockSpec((k, tn), lambda j: (0, j))],
        out_specs=pl.BlockSpec((None, k, tn), lambda j: (j, 0, 0)),
        out_shape=jax.ShapeDtypeStruct((n // tn, k, tn), BF16),
        compiler_params=_params(1),
        name="cast_col_blocks",
    )(w)


def _cconv(x, g, w1, b1, dw_w, dw_b, ln_g, ln_b, w2, b2, hist32, n_seq):
    m, d = x.shape
    sample = hist32 is not None
    nb, _, tn = w2.shape
    hist_rows = CCONV_WIDTH - 1
    tm = CCONV_TOKEN_TILE
    l = m // n_seq
    tiles_per_seq = None if sample else l // tm
    bt = tm // l if sample else None

    def resident(shape):
        return pl.BlockSpec(shape, lambda i: (0,) * len(shape), pipeline_mode=pl.Buffered(1))

    in_specs = [
        pl.BlockSpec((tm, d), lambda i: (i, 0)),
        resident((1, d)),
        resident((2 * nb, d, tn)),
        resident((1, 2 * d)),
        resident((CCONV_WIDTH, d)),
        resident((1, d)),
        resident((1, d)),
        resident((1, d)),
        resident((nb, d, tn)),
        resident((1, d)),
    ]
    args = [x, g, w1, b1, dw_w, dw_b, ln_g, ln_b, w2, b2]
    if sample:
        in_specs.append(pl.BlockSpec((bt, CCONV_HALO, d), lambda i: (i, 0, 0), pipeline_mode=pl.Buffered(1)))
        args.append(hist32)
        st_spec = pl.BlockSpec((bt, hist_rows, d), lambda i: (i, 0, 0))
        scratch = [pltpu.VMEM((bt, l + CCONV_HALO, tn), F32), pltpu.VMEM((tm, d), F32)]
    else:
        st_spec = pl.BlockSpec((n_seq, nb, hist_rows, tn), lambda i: (0, 0, 0, 0))
        scratch = [pltpu.VMEM((nb, tm + CCONV_HALO, tn), F32), pltpu.VMEM((tm, d), F32),
                   pltpu.VMEM((nb, CCONV_HALO, tn), F32)]
    st_shape = (n_seq, hist_rows, d) if sample else (n_seq, nb, hist_rows, tn)
    out, state = pl.pallas_call(
        functools.partial(_cconv_kernel, sample=sample, tiles_per_seq=tiles_per_seq, nb=nb),
        grid=(m // tm,),
        in_specs=in_specs,
        out_specs=[pl.BlockSpec((tm, d), lambda i: (i, 0)), st_spec],
        out_shape=[jax.ShapeDtypeStruct((m, d), F32),
                   jax.ShapeDtypeStruct(st_shape, F32)],
        scratch_shapes=scratch,
        compiler_params=_params(1),
        name="cconv_sample" if sample else "cconv_prompt",
    )(*args)
    return out, _merge_col_blocks(state)


def _pad_front(hist, rows):
    return jnp.pad(hist, ((0, 0), (rows - hist.shape[1], 0), (0, 0)))


def _trunk(x, pool_hist, sconv_hist, cconv_hist, pos0, p, ffn_weights):
    b, s, d = x.shape
    sample = pool_hist is not None
    row = lambda v: v.reshape(1, -1)

    if sample:
        x, pool_state = _pool_sample(x, _pad_front(pool_hist, POOL_HALO), row(p['norm_mix_g'][0]),
                                     p['pool_w'], row(p['pool_scale']), pos0)
    else:
        x, pool_state = _pool_prompt(x, row(p['norm_mix_g'][0]), p['pool_w'], row(p['pool_scale']), pos0)
    x = x.reshape(b * s, d)
    used_weights = []

    def ffn(x, layer, g_final=None):
        if ffn_weights is None:
            w = (p['ffn_w_gate'], p['ffn_w_up'], p['ffn_w_down'])
        else:
            w = ffn_weights[layer]
        out, w_bf16 = _ffn(x, row(p['norm_ffn_g'][layer]), *w, layer, g_final)
        used_weights.append(w_bf16 or w)
        return out

    x = ffn(x, 0)

    if sample:
        reps = GMLP_CHUNK // s
        ws = jnp.tile(p['gmlp_w_s'][:, :s, :s], (1, reps, reps))
        bs = jnp.tile(p['gmlp_b_s'][:, :s], (1, reps))
        block_diag = s
    else:
        ws, bs, block_diag = p['gmlp_w_s'], p['gmlp_b_s'], 0
    x, v = _gmlp(x, row(p['norm_mix_g'][1]), p['gmlp_w_in'], row(p['gmlp_b_in']), row(p['gmlp_ln_g']),
                 row(p['gmlp_ln_b']), ws, bs[:, :, None], p['gmlp_w_out'], block_diag, emit_v=sample)
    x = ffn(x, 1)

    hist8 = _pad_front(sconv_hist, SCONV_HALO) if sample else None
    x, sconv_state = _sconv(x, row(p['norm_mix_g'][2]), p['sconv_w_in'], p['sconv_conv_w'],
                            p['sconv_w_out'], hist8, b)
    x = ffn(x, 2)

    hist32 = _pad_front(cconv_hist, CCONV_HALO) if sample else None
    x, cconv_state = _cconv(x, row(p['norm_mix_g'][3]), p['cconv_w_pw1'], row(p['cconv_b_pw1']),
                            p['cconv_dw_w'], row(p['cconv_dw_b']), row(p['cconv_ln_g']),
                            row(p['cconv_ln_b']), p['cconv_w_pw2'], row(p['cconv_b_pw2']), hist32, b)
    y = ffn(x, 3, row(p['norm_final_g']))
    gmlp_v = v.reshape(b, s, -1) if sample else None
    return y.reshape(b, s, d), pool_state, gmlp_v, sconv_state, cconv_state, used_weights


MIXER_MATMUL_WEIGHTS = ('pool_w',)
COL_BLOCKED_WEIGHTS = ('gmlp_w_in', 'gmlp_w_out', 'sconv_w_in', 'sconv_w_out', 'cconv_w_pw1', 'cconv_w_pw2')


def kernel(x_prompt, x_sample, state_pool, state_sconv, state_cconv, norm_mix_g, norm_ffn_g, norm_final_g, pool_w, pool_scale, gmlp_w_in, gmlp_b_in, gmlp_ln_g, gmlp_ln_b, gmlp_w_s, gmlp_b_s, gmlp_w_out, sconv_w_in, sconv_conv_w, sconv_w_out, cconv_w_pw1, cconv_b_pw1, cconv_dw_w, cconv_dw_b, cconv_ln_g, cconv_ln_b, cconv_w_pw2, cconv_b_pw2, ffn_w_gate, ffn_w_up, ffn_w_down):
    p = dict(norm_mix_g=norm_mix_g, norm_ffn_g=norm_ffn_g, norm_final_g=norm_final_g,
             pool_w=pool_w, pool_scale=pool_scale,
             gmlp_w_in=gmlp_w_in, gmlp_b_in=gmlp_b_in, gmlp_ln_g=gmlp_ln_g, gmlp_ln_b=gmlp_ln_b,
             gmlp_w_s=gmlp_w_s, gmlp_b_s=gmlp_b_s, gmlp_w_out=gmlp_w_out,
             sconv_w_in=sconv_w_in, sconv_conv_w=sconv_conv_w, sconv_w_out=sconv_w_out,
             cconv_w_pw1=cconv_w_pw1, cconv_b_pw1=cconv_b_pw1, cconv_dw_w=cconv_dw_w, cconv_dw_b=cconv_dw_b,
             cconv_ln_g=cconv_ln_g, cconv_ln_b=cconv_ln_b, cconv_w_pw2=cconv_w_pw2, cconv_b_pw2=cconv_b_pw2,
             ffn_w_gate=ffn_w_gate, ffn_w_up=ffn_w_up, ffn_w_down=ffn_w_down)
    for name in MIXER_MATMUL_WEIGHTS:
        p[name] = p[name].astype(BF16)
    for name in COL_BLOCKED_WEIGHTS:
        p[name] = _col_blocks_bf16(p[name], COL_TILE)
    y_s, pool_s, gmlp_v_s, sconv_s, cconv_s, ffn_weights = _trunk(
        x_sample, state_pool, state_sconv, state_cconv, PAST_LEN, p, None)
    y_p, pool_p, _, sconv_p, cconv_p, _ = _trunk(x_prompt, None, None, None, 0, p, ffn_weights)
    return (y_p, y_s, pool_p, pool_s, gmlp_v_s, sconv_p, sconv_s, cconv_p, cconv_s)
```

```python
import functools

import jax
import jax.numpy as jnp
from jax import lax
from jax.experimental import pallas as pl
from jax.experimental.pallas import tpu as pltpu

F32 = jnp.float32
BF16 = jnp.bfloat16

EPS = 1e-6
PAST_LEN = 4096
POOL_WINDOWS = (2, 4, 8, 16)
POOL_HIST = max(POOL_WINDOWS) - 1
POOL_HALO = 16
GMLP_CHUNK = 128
GMLP_CAUSAL_SHIFT = 6
GMLP_HEADS = 8
GMLP_TOKEN_TILE = 256
SCONV_TOKEN_TILE = 256
SCONV_WIDTH = 3
SCONV_HALO = 8
CCONV_WIDTH = 31
CCONV_HALO = 32

SUBLANES = 8
TOKEN_TILE = 512
CCONV_TOKEN_TILE = 256
FFN_TOKEN_TILE = 1024
FFN_CAST_TOKEN_TILE = 512
FFN_CAST_COL_TILE = 512
COL_TILE = 512
VMEM_LIMIT_BYTES = 56 * 1024 * 1024


def _rms(x, g):
    ms = jnp.mean(x * x, axis=-1, keepdims=True)
    return x * lax.rsqrt(ms + EPS) * g


def _layernorm(x, g, b):
    mu = jnp.mean(x, axis=-1, keepdims=True)
    xc = x - mu
    var = jnp.mean(xc * xc, axis=-1, keepdims=True)
    return xc * lax.rsqrt(var + EPS) * g + b


def _dot(a, b):
    return jnp.dot(a, b, preferred_element_type=F32)


def _params(n_axes):
    return pltpu.CompilerParams(
        dimension_semantics=("arbitrary",) * n_axes,
        vmem_limit_bytes=VMEM_LIMIT_BYTES)


def _ffn_kernel(x_ref, g_ref, wg_ref, wu_ref, wd_ref, *rest, final_norm):
    rest = list(rest)
    gf_ref = rest.pop(0) if final_norm else None
    o_ref, wg_out_ref, wu_out_ref, wd_out_ref, h_ref = rest
    k = pl.program_id(1)

    @pl.when(k == 0)
    def _():
        x = x_ref[...]
        h_ref[...] = _rms(x, g_ref[...]).astype(BF16)
        o_ref[...] = x

    wg, wu, wd = wg_ref[...].astype(BF16), wu_ref[...].astype(BF16), wd_ref[...].astype(BF16)
    wg_out_ref[...] = wg
    wu_out_ref[...] = wu
    wd_out_ref[...] = wd
    h = h_ref[...]
    gate = _dot(h, wg)
    up = _dot(h, wu)
    act = (gate * jax.nn.sigmoid(gate) * up).astype(BF16)
    o_ref[...] += _dot(act, wd)

    if final_norm:
        @pl.when(k == pl.num_programs(1) - 1)
        def _():
            o_ref[...] = _rms(o_ref[...], gf_ref[...])


def _ffn_loop_kernel(x_hbm, g_ref, wg_hbm, wu_hbm, wd_hbm, *rest, final_norm):
    rest = list(rest)
    gf_ref = rest.pop(0) if final_norm else None
    o_ref, h_ref, x_tile, wg_buf, wu_buf, wd_buf, x_sem, w_sem = rest
    i = pl.program_id(0)
    n_tiles = pl.num_programs(0)
    tm = x_tile.shape[0]
    tf = wg_buf.shape[2]
    nk = wg_hbm.shape[1] // tf

    def fetch_x(tile):
        return pltpu.make_async_copy(x_hbm.at[pl.ds(tile * tm, tm)], x_tile, x_sem)

    def fetch_w(k, slot):
        cols = pl.ds(pl.multiple_of(k * tf, tf), tf)
        return (pltpu.make_async_copy(wg_hbm.at[:, cols], wg_buf.at[slot], w_sem.at[0, slot]),
                pltpu.make_async_copy(wu_hbm.at[:, cols], wu_buf.at[slot], w_sem.at[1, slot]),
                pltpu.make_async_copy(wd_hbm.at[cols, :], wd_buf.at[slot], w_sem.at[2, slot]))

    first_slot = (i * nk) % 2

    @pl.when(i == 0)
    def _():
        fetch_x(0).start()
        for c in fetch_w(0, 0):
            c.start()

    def advance(k):
        slot = (first_slot + k) % 2
        for c in fetch_w(k, slot):
            c.wait()
        more_here = k + 1 < nk

        @pl.when(more_here | (i + 1 < n_tiles))
        def _():
            for c in fetch_w(jnp.where(more_here, k + 1, 0), 1 - slot):
                c.start()

        return slot

    def hidden_block(h, slot):
        half = tf // 2
        acc = None
        for c in range(2):
            cols = slice(c * half, (c + 1) * half)
            gate = _dot(h, wg_buf[slot, :, cols])
            up = _dot(h, wu_buf[slot, :, cols])
            act = (gate * jax.nn.sigmoid(gate) * up).astype(BF16)
            part = _dot(act, wd_buf[slot, cols, :])
            acc = part if acc is None else acc + part
        return acc

    fetch_x(i).wait()
    slot0 = advance(0)
    for r in range(2):
        rows = slice(r * (tm // 2), (r + 1) * (tm // 2))
        h = _rms(x_tile[rows, :], g_ref[...]).astype(BF16)
        h_ref[rows, :] = h
        o_ref[rows, :] = x_tile[rows, :] + hidden_block(h, slot0)

    def step(k, carry):
        slot = advance(k)

        @pl.when((k == 1) & (i + 1 < n_tiles))
        def _():
            fetch_x(i + 1).start()

        o_ref[...] += hidden_block(h_ref[...], slot)
        return carry

    lax.fori_loop(1, nk, step, 0)

    if final_norm:
        o_ref[...] = _rms(o_ref[...], gf_ref[...])


def _ffn(x, g, wg, wu, wd, layer, g_final=None):
    m, d = x.shape
    emit_bf16 = wg.ndim == 3
    f = wg.shape[-1]
    if emit_bf16:
        tm, tf = FFN_CAST_TOKEN_TILE, FFN_CAST_COL_TILE
        assert m == tm, "every weight block must be visited exactly once"
        w_specs = [
            pl.BlockSpec((None, d, tf), lambda i, k: (layer, 0, k)),
            pl.BlockSpec((None, d, tf), lambda i, k: (layer, 0, k)),
            pl.BlockSpec((None, tf, d), lambda i, k: (layer, k, 0)),
        ]
    else:
        tm, tf = FFN_TOKEN_TILE, COL_TILE
    final_norm = g_final is not None
    if not emit_bf16:
        any_spec = pl.BlockSpec(memory_space=pl.ANY)
        in_specs = [any_spec, pl.BlockSpec((1, d), lambda i: (0, 0)), any_spec, any_spec, any_spec]
        args = [x, g, wg, wu, wd]
        if final_norm:
            in_specs.append(pl.BlockSpec((1, d), lambda i: (0, 0)))
            args.append(g_final)
        out = pl.pallas_call(
            functools.partial(_ffn_loop_kernel, final_norm=final_norm),
            grid=(m // tm,),
            in_specs=in_specs,
            out_specs=pl.BlockSpec((tm, d), lambda i: (i, 0)),
            out_shape=jax.ShapeDtypeStruct((m, d), F32),
            scratch_shapes=[pltpu.VMEM((tm, d), BF16), pltpu.VMEM((tm, d), F32),
                            pltpu.VMEM((2, d, tf), BF16), pltpu.VMEM((2, d, tf), BF16),
                            pltpu.VMEM((2, tf, d), BF16),
                            pltpu.SemaphoreType.DMA(()), pltpu.SemaphoreType.DMA((3, 2))],
            compiler_params=_params(1),
            name="ffn_final" if final_norm else "ffn",
        )(*args)
        return out, ()
    in_specs = [pl.BlockSpec((tm, d), lambda i, k: (i, 0), pipeline_mode=pl.Buffered(1)),
                pl.BlockSpec((1, d), lambda i, k: (0, 0))] + w_specs
    args = [x, g, wg, wu, wd]
    if final_norm:
        in_specs.append(pl.BlockSpec((1, d), lambda i, k: (0, 0)))
        args.append(g_final)
    out_specs = [pl.BlockSpec((tm, d), lambda i, k: (i, 0)),
                 pl.BlockSpec((d, tf), lambda i, k: (0, k)),
                 pl.BlockSpec((d, tf), lambda i, k: (0, k)),
                 pl.BlockSpec((tf, d), lambda i, k: (k, 0))]
    out_shape = [jax.ShapeDtypeStruct((m, d), F32), jax.ShapeDtypeStruct((d, f), BF16),
                 jax.ShapeDtypeStruct((d, f), BF16), jax.ShapeDtypeStruct((f, d), BF16)]
    outs = pl.pallas_call(
        functools.partial(_ffn_kernel, final_norm=final_norm),
        grid=(m // tm, f // tf),
        in_specs=in_specs,
        out_specs=out_specs,
        out_shape=out_shape,
        scratch_shapes=[pltpu.VMEM((tm, d), BF16)],
        compiler_params=_params(2),
        name="ffn_cast" + ("_final" if final_norm else ""),
    )(*args)
    return outs[0], tuple(outs[1:])


def _pool_body(x_ref, hist, g_ref, pw_ref, sc_ref, o_ref, xe_ref, pos_first):
    bt, l, d = x_ref.shape
    grp = d // len(POOL_WINDOWS)
    x = x_ref[...]
    xe_ref[:, 0:POOL_HALO, :] = hist
    xe_ref[:, POOL_HALO:, :] = _rms(x, g_ref[...])
    pos = lax.broadcasted_iota(jnp.int32, (1, l, 1), 1) + pos_first
    for gi, w in enumerate(POOL_WINDOWS):
        c0 = gi * grp
        hg = xe_ref[:, POOL_HALO:, c0:c0 + grp]
        if bt == 1:
            a = xe_ref[0, :, c0:c0 + grp]
            span = 1
            while span < min(w, SUBLANES):
                a = a + pltpu.roll(a, span, axis=0)
                span *= 2
            acc = a[POOL_HALO:POOL_HALO + l]
            if w > SUBLANES:
                assert w == 2 * SUBLANES
                acc = acc + a[POOL_HALO - SUBLANES:POOL_HALO - SUBLANES + l]
            acc = acc[None]
        else:
            acc = hg
            for k in range(1, w):
                acc = acc + xe_ref[:, POOL_HALO - k:POOL_HALO - k + l, c0:c0 + grp]
        inv_cnt = 1.0 / jnp.minimum(pos + 1, w).astype(F32)
        diff = (acc * inv_cnt - hg).reshape(bt * l, grp).astype(BF16)
        y = _dot(diff, pw_ref[gi]) * sc_ref[:, c0:c0 + grp]
        o_ref[:, :, c0:c0 + grp] = x_ref[:, :, c0:c0 + grp] + y.reshape(bt, l, grp)


def _pool_prompt_kernel(x_ref, g_ref, pw_ref, sc_ref, o_ref, st_ref, xe_ref, carry_ref, *, pos0):
    i = pl.program_id(1)
    l = x_ref.shape[1]

    @pl.when(i == 0)
    def _():
        carry_ref[...] = jnp.zeros_like(carry_ref)

    _pool_body(x_ref, carry_ref[...], g_ref, pw_ref, sc_ref, o_ref, xe_ref, pos0 + i * l)
    carry_ref[...] = xe_ref[:, l:l + POOL_HALO, :]
    st_ref[...] = xe_ref[:, l + POOL_HALO - POOL_HIST:l + POOL_HALO, :]


def _pool_sample_kernel(x_ref, hist_ref, g_ref, pw_ref, sc_ref, o_ref, st_ref, xe_ref, *, pos0):
    l = x_ref.shape[1]
    _pool_body(x_ref, hist_ref[...], g_ref, pw_ref, sc_ref, o_ref, xe_ref, pos0)
    st_ref[...] = xe_ref[:, l + POOL_HALO - POOL_HIST:l + POOL_HALO, :]


def _pool_prompt(x, g, pw, scale, pos0):
    b, s, d = x.shape
    ts = TOKEN_TILE
    ng = len(POOL_WINDOWS)
    return pl.pallas_call(
        functools.partial(_pool_prompt_kernel, pos0=pos0),
        grid=(b, s // ts),
        in_specs=[
            pl.BlockSpec((1, ts, d), lambda bi, i: (bi, i, 0)),
            pl.BlockSpec((1, d), lambda bi, i: (0, 0)),
            pl.BlockSpec((ng, d // ng, d // ng), lambda bi, i: (0, 0, 0)),
            pl.BlockSpec((1, d), lambda bi, i: (0, 0)),
        ],
        out_specs=[
            pl.BlockSpec((1, ts, d), lambda bi, i: (bi, i, 0)),
            pl.BlockSpec((1, POOL_HIST, d), lambda bi, i: (bi, 0, 0)),
        ],
        out_shape=[jax.ShapeDtypeStruct((b, s, d), F32),
                   jax.ShapeDtypeStruct((b, POOL_HIST, d), F32)],
        scratch_shapes=[pltpu.VMEM((1, ts + POOL_HALO, d), F32),
                        pltpu.VMEM((1, POOL_HALO, d), F32)],
        compiler_params=_params(2),
        name="pool_prompt",
    )(x, g, pw, scale)


def _pool_sample(x, hist16, g, pw, scale, pos0):
    b, s, d = x.shape
    bt = 8
    ng = len(POOL_WINDOWS)
    return pl.pallas_call(
        functools.partial(_pool_sample_kernel, pos0=pos0),
        grid=(b // bt,),
        in_specs=[
            pl.BlockSpec((bt, s, d), lambda i: (i, 0, 0)),
            pl.BlockSpec((bt, POOL_HALO, d), lambda i: (i, 0, 0)),
            pl.BlockSpec((1, d), lambda i: (0, 0)),
            pl.BlockSpec((ng, d // ng, d // ng), lambda i: (0, 0, 0)),
            pl.BlockSpec((1, d), lambda i: (0, 0)),
        ],
        out_specs=[
            pl.BlockSpec((bt, s, d), lambda i: (i, 0, 0)),
            pl.BlockSpec((bt, POOL_HIST, d), lambda i: (i, 0, 0)),
        ],
        out_shape=[jax.ShapeDtypeStruct((b, s, d), F32),
                   jax.ShapeDtypeStruct((b, POOL_HIST, d), F32)],
        scratch_shapes=[pltpu.VMEM((bt, s + POOL_HALO, d), F32)],
        compiler_params=_params(1),
        name="pool_sample",
    )(x, hist16, g, pw, scale)


def _gmlp_kernel(x_ref, g_ref, win_ref, bin_ref, lng_ref, lnb_ref, ws_ref, bs_ref, wout_ref, o_ref, *rest,
                 block_diag, emit_v):
    if emit_v:
        vout_ref, z_ref, t_ref = rest
    else:
        z_ref, t_ref = rest
    tm = x_ref.shape[0]
    n_in, _, tn = win_ref.shape
    n_out, _, tno = wout_ref.shape
    width = n_in * tn // 2
    hdim = width // GMLP_HEADS

    h = _rms(x_ref[...], g_ref[...]).astype(BF16)
    for n in range(n_in):
        cols = slice(n * tn, (n + 1) * tn)
        z_ref[:, cols] = jax.nn.gelu(_dot(h, win_ref[n]) + bin_ref[:, cols])
    v = _layernorm(z_ref[:, width:], lng_ref[...], lnb_ref[...])
    if emit_v:
        vout_ref[...] = v
    v = v.astype(BF16)

    row = lax.broadcasted_iota(jnp.int32, (GMLP_CHUNK, GMLP_CHUNK), 0)
    col = lax.broadcasted_iota(jnp.int32, (GMLP_CHUNK, GMLP_CHUNK), 1)
    if block_diag:
        shift = block_diag.bit_length() - 1
        mask = (row >> shift) == (col >> shift)
    else:
        mask = (col >> GMLP_CAUSAL_SHIFT) <= (row >> GMLP_CAUSAL_SHIFT)
    for head in range(GMLP_HEADS):
        lanes = slice(head * hdim, (head + 1) * hdim)
        wm = jnp.where(mask, ws_ref[head], 0.0).astype(BF16)
        bias = bs_ref[head]
        for c in range(tm // GMLP_CHUNK):
            rows = slice(c * GMLP_CHUNK, (c + 1) * GMLP_CHUNK)
            s = _dot(wm, v[rows, lanes]) + bias
            t_ref[rows, lanes] = (z_ref[rows, lanes] * s).astype(BF16)
    t = t_ref[...]
    for n in range(n_out):
        cols = slice(n * tno, (n + 1) * tno)
        o_ref[:, cols] = x_ref[:, cols] + _dot(t, wout_ref[n])


def _gmlp(x, g, w_in, b_in, ln_g, ln_b, ws, bs, w_out, block_diag, emit_v):
    m, d = x.shape
    n_in, _, tn = w_in.shape
    width = n_in * tn // 2
    tm = GMLP_TOKEN_TILE

    def resident(shape):
        return pl.BlockSpec(shape, lambda i: (0,) * len(shape), pipeline_mode=pl.Buffered(1))

    out_specs = [pl.BlockSpec((tm, d), lambda i: (i, 0))]
    out_shape = [jax.ShapeDtypeStruct((m, d), F32)]
    if emit_v:
        out_specs.append(pl.BlockSpec((tm, width), lambda i: (i, 0)))
        out_shape.append(jax.ShapeDtypeStruct((m, width), F32))
    outs = pl.pallas_call(
        functools.partial(_gmlp_kernel, block_diag=block_diag, emit_v=emit_v),
        grid=(m // tm,),
        in_specs=[
            pl.BlockSpec((tm, d), lambda i: (i, 0)),
            resident((1, d)),
            resident(w_in.shape),
            resident((1, 2 * width)),
            resident((1, width)),
            resident((1, width)),
            resident((GMLP_HEADS, GMLP_CHUNK, GMLP_CHUNK)),
            resident((GMLP_HEADS, GMLP_CHUNK, 1)),
            resident(w_out.shape),
        ],
        out_specs=out_specs,
        out_shape=out_shape,
        scratch_shapes=[pltpu.VMEM((tm, 2 * width), F32),
                        pltpu.VMEM((tm, width), BF16)],
        compiler_params=_params(1),
        name="gmlp_v" if emit_v else "gmlp",
    )(x, g, w_in, b_in, ln_g, ln_b, ws, bs, w_out)
    return outs if emit_v else (outs[0], None)


def _merge_col_blocks(state):
    if state.ndim == 3:
        return state
    n_seq, nb, rows, tn = state.shape
    return state.transpose(0, 2, 1, 3).reshape(n_seq, rows, nb * tn)


def _sconv_kernel(x_ref, g_ref, win_ref, cw_ref, wout_ref, *rest, sample, tiles_per_seq):
    if sample:
        hist_ref, o_ref, st_ref, cxe_ref, gated_ref = rest
    else:
        o_ref, st_ref, cxe_ref, gated_ref, carry_ref = rest
    i = pl.program_id(0)
    tm = x_ref.shape[0]
    nb, _, tn = wout_ref.shape
    halo = SCONV_HALO
    first_tap = halo - (SCONV_WIDTH - 1)

    h = _rms(x_ref[...], g_ref[...]).astype(BF16)

    if not sample:
        @pl.when(i % tiles_per_seq == 0)
        def _():
            carry_ref[...] = jnp.zeros_like(carry_ref)

    for n in range(nb):
        cols = slice(n * tn, (n + 1) * tn)
        b_gate = _dot(h, win_ref[n])
        cx = _dot(h, win_ref[nb + n]) * _dot(h, win_ref[2 * nb + n])
        if sample:
            bt, l = cxe_ref.shape[0], cxe_ref.shape[1] - halo
            cxe_ref[:, 0:halo, :] = hist_ref[:, :, cols]
            cxe_ref[:, halo:, :] = cx.reshape(bt, l, tn)
            conv = None
            for k in range(SCONV_WIDTH):
                term = cw_ref[k:k + 1, cols] * cxe_ref[:, first_tap + k:first_tap + k + l, :]
                conv = term if conv is None else conv + term
            conv = conv.reshape(tm, tn)
            st_ref[:, :, cols] = cxe_ref[:, l + first_tap:l + halo, :]
        else:
            cxe_ref[n, 0:halo, :] = carry_ref[n]
            cxe_ref[n, halo:, :] = cx
            carry_ref[n] = cxe_ref[n, tm:tm + halo, :]
            conv = None
            for k in range(SCONV_WIDTH):
                term = cw_ref[k:k + 1, cols] * cxe_ref[n, first_tap + k:first_tap + k + tm, :]
                conv = term if conv is None else conv + term
            st_ref[i // tiles_per_seq, n] = cxe_ref[n, tm + first_tap:tm + halo, :]
        gated_ref[:, cols] = (b_gate * conv).astype(BF16)

    gated = gated_ref[...]
    for n in range(nb):
        cols = slice(n * tn, (n + 1) * tn)
        o_ref[:, cols] = x_ref[:, cols] + _dot(gated, wout_ref[n])


def _sconv(x, g, w_in, conv_w, w_out, hist8, n_seq):
    m, d = x.shape
    sample = hist8 is not None
    nb, _, tn = w_out.shape
    hist_rows = SCONV_WIDTH - 1
    tm = SCONV_TOKEN_TILE
    l = m // n_seq
    tiles_per_seq = None if sample else l // tm
    bt = tm // l if sample else None

    def resident(shape):
        return pl.BlockSpec(shape, lambda i: (0,) * len(shape), pipeline_mode=pl.Buffered(1))

    in_specs = [
        pl.BlockSpec((tm, d), lambda i: (i, 0)),
        resident((1, d)),
        resident(w_in.shape),
        resident((SCONV_WIDTH, d)),
        resident(w_out.shape),
    ]
    args = [x, g, w_in, conv_w, w_out]
    if sample:
        in_specs.append(pl.BlockSpec((bt, SCONV_HALO, d), lambda i: (i, 0, 0)))
        args.append(hist8)
        st_spec = pl.BlockSpec((bt, hist_rows, d), lambda i: (i, 0, 0))
        scratch = [pltpu.VMEM((bt, l + SCONV_HALO, tn), F32), pltpu.VMEM((tm, d), BF16)]
    else:
        st_spec = pl.BlockSpec((n_seq, nb, hist_rows, tn), lambda i: (0, 0, 0, 0))
        scratch = [pltpu.VMEM((nb, tm + SCONV_HALO, tn), F32), pltpu.VMEM((tm, d), BF16),
                   pltpu.VMEM((nb, SCONV_HALO, tn), F32)]
    st_shape = (n_seq, hist_rows, d) if sample else (n_seq, nb, hist_rows, tn)
    out, state = pl.pallas_call(
        functools.partial(_sconv_kernel, sample=sample, tiles_per_seq=tiles_per_seq),
        grid=(m // tm,),
        in_specs=in_specs,
        out_specs=[pl.BlockSpec((tm, d), lambda i: (i, 0)), st_spec],
        out_shape=[jax.ShapeDtypeStruct((m, d), F32),
                   jax.ShapeDtypeStruct(st_shape, F32)],
        scratch_shapes=scratch,
        compiler_params=_params(1),
        name="sconv_sample" if sample else "sconv_prompt",
    )(*args)
    return out, _merge_col_blocks(state)


def _causal_taps_flat(glue_ref, n, w_ref, c0, tn, tm, halo, width):
    first_tap = halo - (width - 1)
    ext = tm + SUBLANES
    conv = None
    for r in range(SUBLANES):
        part = None
        for o in range(first_tap, first_tap + width):
            if o % SUBLANES != r:
                continue
            k = o - first_tap
            rows = tm if r == 0 else ext
            term = w_ref[k:k + 1, c0:c0 + tn] * glue_ref[n, o - r:o - r + rows, :]
            part = term if part is None else part + term
        if part is None:
            continue
        if r:
            part = pltpu.roll(part, ext - r, axis=0)[:tm]
        conv = part if conv is None else conv + part
    return conv


def _cconv_kernel(x_ref, g_ref, w1_ref, b1_ref, dw_ref, dwb_ref, lng_ref, lnb_ref, w2_ref, b2_ref,
                  *rest, sample, tiles_per_seq, nb):
    if sample:
        hist_ref, o_ref, st_ref, glue_ref, conv_ref = rest
    else:
        o_ref, st_ref, glue_ref, conv_ref, carry_ref = rest
    i = pl.program_id(0)
    tm, d = x_ref.shape
    tn = d // nb
    halo = CCONV_HALO
    first_tap = halo - (CCONV_WIDTH - 1)

    x = x_ref[...]
    h = _rms(x, g_ref[...]).astype(BF16)

    if not sample:
        @pl.when(i % tiles_per_seq == 0)
        def _():
            carry_ref[...] = jnp.zeros_like(carry_ref)

    for n in range(nb):
        c0 = n * tn
        a = _dot(h, w1_ref[n]) + b1_ref[:, c0:c0 + tn]
        gt = _dot(h, w1_ref[nb + n]) + b1_ref[:, d + c0:d + c0 + tn]
        glu = a * jax.nn.sigmoid(gt)
        if sample:
            bt, l = glue_ref.shape[0], glue_ref.shape[1] - halo
            glue_ref[:, 0:halo, :] = hist_ref[:, :, c0:c0 + tn]
            glue_ref[:, halo:, :] = glu.reshape(bt, l, tn)
            conv = None
            for k in range(CCONV_WIDTH):
                term = dw_ref[k:k + 1, c0:c0 + tn] * glue_ref[:, first_tap + k:first_tap + k + l, :]
                conv = term if conv is None else conv + term
            conv = conv.reshape(tm, tn)
            st_ref[:, :, c0:c0 + tn] = glue_ref[:, l + first_tap:l + halo, :]
        else:
            glue_ref[n, 0:halo, :] = carry_ref[n]
            glue_ref[n, halo:, :] = glu
            carry_ref[n] = glue_ref[n, tm:tm + halo, :]
            conv = _causal_taps_flat(glue_ref, n, dw_ref, c0, tn, tm, halo, CCONV_WIDTH)
            st_ref[i // tiles_per_seq, n] = glue_ref[n, tm + first_tap:tm + halo, :]
        conv_ref[:, c0:c0 + tn] = conv + dwb_ref[:, c0:c0 + tn]

    y = _layernorm(conv_ref[...], lng_ref[...], lnb_ref[...])
    z = (y * jax.nn.sigmoid(y)).astype(BF16)
    for n in range(nb):
        cols = slice(n * tn, (n + 1) * tn)
        o_ref[:, cols] = x_ref[:, cols] + _dot(z, w2_ref[n]) + b2_ref[:, cols]


def _cast_kernel(w_ref, o_ref):
    o_ref[...] = w_ref[...].astype(o_ref.dtype)


def _col_blocks_bf16(w, tn):
    k, n = w.shape
    return pl.pallas_call(
        _cast_kernel,
        grid=(n // tn,),
        in_specs=[pl.BlockSpec((k, tn), lambda j: (0, j))],
        out_specs=pl.BlockSpec((None, k, tn), lambda j: (j, 0, 0)),
        out_shape=jax.ShapeDtypeStruct((n // tn, k, tn), BF16),
        compiler_params=_params(1),
        name="cast_col_blocks",
    )(w)


def _cconv(x, g, w1, b1, dw_w, dw_b, ln_g, ln_b, w2, b2, hist32, n_seq):
    m, d = x.shape
    sample = hist32 is not None
    nb, _, tn = w2.shape
    hist_rows = CCONV_WIDTH - 1
    tm = CCONV_TOKEN_TILE
    l = m // n_seq
    tiles_per_seq = None if sample else l // tm
    bt = tm // l if sample else None

    def resident(shape):
        return pl.BlockSpec(shape, lambda i: (0,) * len(shape), pipeline_mode=pl.Buffered(1))

    in_specs = [
        pl.BlockSpec((tm, d), lambda i: (i, 0)),
        resident((1, d)),
        resident((2 * nb, d, tn)),
        resident((1, 2 * d)),
        resident((CCONV_WIDTH, d)),
        resident((1, d)),
        resident((1, d)),
        resident((1, d)),
        resident((nb, d, tn)),
        resident((1, d)),
    ]
    args = [x, g, w1, b1, dw_w, dw_b, ln_g, ln_b, w2, b2]
    if sample:
        in_specs.append(pl.BlockSpec((bt, CCONV_HALO, d), lambda i: (i, 0, 0), pipeline_mode=pl.Buffered(1)))
        args.append(hist32)
        st_spec = pl.BlockSpec((bt, hist_rows, d), lambda i: (i, 0, 0))
        scratch = [pltpu.VMEM((bt, l + CCONV_HALO, tn), F32), pltpu.VMEM((tm, d), F32)]
    else:
        st_spec = pl.BlockSpec((n_seq, nb, hist_rows, tn), lambda i: (0, 0, 0, 0))
        scratch = [pltpu.VMEM((nb, tm + CCONV_HALO, tn), F32), pltpu.VMEM((tm, d), F32),
                   pltpu.VMEM((nb, CCONV_HALO, tn), F32)]
    st_shape = (n_seq, hist_rows, d) if sample else (n_seq, nb, hist_rows, tn)
    out, state = pl.pallas_call(
        functools.partial(_cconv_kernel, sample=sample, tiles_per_seq=tiles_per_seq, nb=nb),
        grid=(m // tm,),
        in_specs=in_specs,
        out_specs=[pl.BlockSpec((tm, d), lambda i: (i, 0)), st_spec],
        out_shape=[jax.ShapeDtypeStruct((m, d), F32),
                   jax.ShapeDtypeStruct(st_shape, F32)],
        scratch_shapes=scratch,
        compiler_params=_params(1),
        name="cconv_sample" if sample else "cconv_prompt",
    )(*args)
    return out, _merge_col_blocks(state)


def _pad_front(hist, rows):
    return jnp.pad(hist, ((0, 0), (rows - hist.shape[1], 0), (0, 0)))


def _trunk(x, pool_hist, sconv_hist, cconv_hist, pos0, p, ffn_weights):
    b, s, d = x.shape
    sample = pool_hist is not None
    row = lambda v: v.reshape(1, -1)

    if sample:
        x, pool_state = _pool_sample(x, _pad_front(pool_hist, POOL_HALO), row(p['norm_mix_g'][0]),
                                     p['pool_w'], row(p['pool_scale']), pos0)
    else:
        x, pool_state = _pool_prompt(x, row(p['norm_mix_g'][0]), p['pool_w'], row(p['pool_scale']), pos0)
    x = x.reshape(b * s, d)
    used_weights = []

    def ffn(x, layer, g_final=None):
        if ffn_weights is None:
            w = (p['ffn_w_gate'], p['ffn_w_up'], p['ffn_w_down'])
        else:
            w = ffn_weights[layer]
        out, w_bf16 = _ffn(x, row(p['norm_ffn_g'][layer]), *w, layer, g_final)
        used_weights.append(w_bf16 or w)
        return out

    x = ffn(x, 0)

    if sample:
        reps = GMLP_CHUNK // s
        ws = jnp.tile(p['gmlp_w_s'][:, :s, :s], (1, reps, reps))
        bs = jnp.tile(p['gmlp_b_s'][:, :s], (1, reps))
        block_diag = s
    else:
        ws, bs, block_diag = p['gmlp_w_s'], p['gmlp_b_s'], 0
    x, v = _gmlp(x, row(p['norm_mix_g'][1]), p['gmlp_w_in'], row(p['gmlp_b_in']), row(p['gmlp_ln_g']),
                 row(p['gmlp_ln_b']), ws, bs[:, :, None], p['gmlp_w_out'], block_diag, emit_v=sample)
    x = ffn(x, 1)

    hist8 = _pad_front(sconv_hist, SCONV_HALO) if sample else None
    x, sconv_state = _sconv(x, row(p['norm_mix_g'][2]), p['sconv_w_in'], p['sconv_conv_w'],
                            p['sconv_w_out'], hist8, b)
    x = ffn(x, 2)

    hist32 = _pad_front(cconv_hist, CCONV_HALO) if sample else None
    x, cconv_state = _cconv(x, row(p['norm_mix_g'][3]), p['cconv_w_pw1'], row(p['cconv_b_pw1']),
                            p['cconv_dw_w'], row(p['cconv_dw_b']), row(p['cconv_ln_g']),
                            row(p['cconv_ln_b']), p['cconv_w_pw2'], row(p['cconv_b_pw2']), hist32, b)
    y = ffn(x, 3, row(p['norm_final_g']))
    gmlp_v = v.reshape(b, s, -1) if sample else None
    return y.reshape(b, s, d), pool_state, gmlp_v, sconv_state, cconv_state, used_weights


MIXER_MATMUL_WEIGHTS = ('pool_w',)
COL_BLOCKED_WEIGHTS = ('gmlp_w_in', 'gmlp_w_out', 'sconv_w_in', 'sconv_w_out', 'cconv_w_pw1', 'cconv_w_pw2')


def kernel(x_prompt, x_sample, state_pool, state_sconv, state_cconv, norm_mix_g, norm_ffn_g, norm_final_g, pool_w, pool_scale, gmlp_w_in, gmlp_b_in, gmlp_ln_g, gmlp_ln_b, gmlp_w_s, gmlp_b_s, gmlp_w_out, sconv_w_in, sconv_conv_w, sconv_w_out, cconv_w_pw1, cconv_b_pw1, cconv_dw_w, cconv_dw_b, cconv_ln_g, cconv_ln_b, cconv_w_pw2, cconv_b_pw2, ffn_w_gate, ffn_w_up, ffn_w_down):
    p = dict(norm_mix_g=norm_mix_g, norm_ffn_g=norm_ffn_g, norm_final_g=norm_final_g,
             pool_w=pool_w, pool_scale=pool_scale,
             gmlp_w_in=gmlp_w_in, gmlp_b_in=gmlp_b_in, gmlp_ln_g=gmlp_ln_g, gmlp_ln_b=gmlp_ln_b,
             gmlp_w_s=gmlp_w_s, gmlp_b_s=gmlp_b_s, gmlp_w_out=gmlp_w_out,
             sconv_w_in=sconv_w_in, sconv_conv_w=sconv_conv_w, sconv_w_out=sconv_w_out,
             cconv_w_pw1=cconv_w_pw1, cconv_b_pw1=cconv_b_pw1, cconv_dw_w=cconv_dw_w, cconv_dw_b=cconv_dw_b,
             cconv_ln_g=cconv_ln_g, cconv_ln_b=cconv_ln_b, cconv_w_pw2=cconv_w_pw2, cconv_b_pw2=cconv_b_pw2,
             ffn_w_gate=ffn_w_gate, ffn_w_up=ffn_w_up, ffn_w_down=ffn_w_down)
    for name in MIXER_MATMUL_WEIGHTS:
        p[name] = p[name].astype(BF16)
    for name in COL_BLOCKED_WEIGHTS:
        p[name] = _col_blocks_bf16(p[name], COL_TILE)
    y_s, pool_s, gmlp_v_s, sconv_s, cconv_s, ffn_weights = _trunk(
        x_sample, state_pool, state_sconv, state_cconv, PAST_LEN, p, None)
    y_p, pool_p, _, sconv_p, cconv_p, _ = _trunk(x_prompt, None, None, None, 0, p, ffn_weights)
    return (y_p, y_s, pool_p, pool_s, gmlp_v_s, sconv_p, sconv_s, cconv_p, cconv_s)
```

```python
import functools

import jax
import jax.numpy as jnp
from jax import lax
from jax.experimental import pallas as pl
from jax.experimental.pallas import tpu as pltpu

F32 = jnp.float32
BF16 = jnp.bfloat16

EPS = 1e-6
PAST_LEN = 4096
POOL_WINDOWS = (2, 4, 8, 16)
POOL_HIST = max(POOL_WINDOWS) - 1
POOL_HALO = 16
GMLP_CHUNK = 128
GMLP_CAUSAL_SHIFT = 6
GMLP_HEADS = 8
GMLP_TOKEN_TILE = 256
SCONV_TOKEN_TILE = 256
SCONV_WIDTH = 3
SCONV_HALO = 8
CCONV_WIDTH = 31
CCONV_HALO = 32

SUBLANES = 8
TOKEN_TILE = 512
CCONV_TOKEN_TILE = 256
FFN_TOKEN_TILE = 1024
FFN_CAST_TOKEN_TILE = 512
FFN_CAST_COL_TILE = 512
COL_TILE = 512
VMEM_LIMIT_BYTES = 56 * 1024 * 1024


def _rms(x, g):
    ms = jnp.mean(x * x, axis=-1, keepdims=True)
    return x * lax.rsqrt(ms + EPS) * g


def _layernorm(x, g, b):
    mu = jnp.mean(x, axis=-1, keepdims=True)
    xc = x - mu
    var = jnp.mean(xc * xc, axis=-1, keepdims=True)
    return xc * lax.rsqrt(var + EPS) * g + b


def _dot(a, b):
    return jnp.dot(a, b, preferred_element_type=F32)


def _params(n_axes):
    return pltpu.CompilerParams(
        dimension_semantics=("arbitrary",) * n_axes,
        vmem_limit_bytes=VMEM_LIMIT_BYTES)


def _ffn_kernel(x_ref, g_ref, wg_ref, wu_ref, wd_ref, *rest, final_norm):
    rest = list(rest)
    gf_ref = rest.pop(0) if final_norm else None
    o_ref, wg_out_ref, wu_out_ref, wd_out_ref, h_ref = rest
    k = pl.program_id(1)

    @pl.when(k == 0)
    def _():
        x = x_ref[...]
        h_ref[...] = _rms(x, g_ref[...]).astype(BF16)
        o_ref[...] = x

    wg, wu, wd = wg_ref[...].astype(BF16), wu_ref[...].astype(BF16), wd_ref[...].astype(BF16)
    wg_out_ref[...] = wg
    wu_out_ref[...] = wu
    wd_out_ref[...] = wd
    h = h_ref[...]
    gate = _dot(h, wg)
    up = _dot(h, wu)
    act = (gate * jax.nn.sigmoid(gate) * up).astype(BF16)
    o_ref[...] += _dot(act, wd)

    if final_norm:
        @pl.when(k == pl.num_programs(1) - 1)
        def _():
            o_ref[...] = _rms(o_ref[...], gf_ref[...])


def _ffn_loop_kernel(x_hbm, g_ref, wg_hbm, wu_hbm, wd_hbm, *rest, final_norm):
    rest = list(rest)
    gf_ref = rest.pop(0) if final_norm else None
    o_ref, h_ref, x_tile, wg_buf, wu_buf, wd_buf, x_sem, w_sem = rest
    i = pl.program_id(0)
    n_tiles = pl.num_programs(0)
    tm = x_tile.shape[0]
    tf = wg_buf.shape[2]
    nk = wg_hbm.shape[1] // tf

    def fetch_x(tile):
        return pltpu.make_async_copy(x_hbm.at[pl.ds(tile * tm, tm)], x_tile, x_sem)

    def fetch_w(k, slot):
        cols = pl.ds(pl.multiple_of(k * tf, tf), tf)
        return (pltpu.make_async_copy(wg_hbm.at[:, cols], wg_buf.at[slot], w_sem.at[0, slot]),
                pltpu.make_async_copy(wu_hbm.at[:, cols], wu_buf.at[slot], w_sem.at[1, slot]),
                pltpu.make_async_copy(wd_hbm.at[cols, :], wd_buf.at[slot], w_sem.at[2, slot]))

    first_slot = (i * nk) % 2

    @pl.when(i == 0)
    def _():
        fetch_x(0).start()
        for c in fetch_w(0, 0):
            c.start()

    def advance(k):
        slot = (first_slot + k) % 2
        for c in fetch_w(k, slot):
            c.wait()
        more_here = k + 1 < nk

        @pl.when(more_here | (i + 1 < n_tiles))
        def _():
            for c in fetch_w(jnp.where(more_here, k + 1, 0), 1 - slot):
                c.start()

        return slot

    def hidden_block(h, slot):
        half = tf // 2
        acc = None
        for c in range(2):
            cols = slice(c * half, (c + 1) * half)
            gate = _dot(h, wg_buf[slot, :, cols])
            up = _dot(h, wu_buf[slot, :, cols])
            act = (gate * jax.nn.sigmoid(gate) * up).astype(BF16)
            part = _dot(act, wd_buf[slot, cols, :])
            acc = part if acc is None else acc + part
        return acc

    fetch_x(i).wait()
    slot0 = advance(0)
    for r in range(2):
        rows = slice(r * (tm // 2), (r + 1) * (tm // 2))
        h = _rms(x_tile[rows, :], g_ref[...]).astype(BF16)
        h_ref[rows, :] = h
        o_ref[rows, :] = x_tile[rows, :] + hidden_block(h, slot0)

    def step(k, carry):
        slot = advance(k)

        @pl.when((k == 1) & (i + 1 < n_tiles))
        def _():
            fetch_x(i + 1).start()

        o_ref[...] += hidden_block(h_ref[...], slot)
        return carry

    lax.fori_loop(1, nk, step, 0)

    if final_norm:
        o_ref[...] = _rms(o_ref[...], gf_ref[...])


def _ffn(x, g, wg, wu, wd, layer, g_final=None):
    m, d = x.shape
    emit_bf16 = wg.ndim == 3
    f = wg.shape[-1]
    if emit_bf16:
        tm, tf = FFN_CAST_TOKEN_TILE, FFN_CAST_COL_TILE
        assert m == tm, "every weight block must be visited exactly once"
        w_specs = [
            pl.BlockSpec((None, d, tf), lambda i, k: (layer, 0, k)),
            pl.BlockSpec((None, d, tf), lambda i, k: (layer, 0, k)),
            pl.BlockSpec((None, tf, d), lambda i, k: (layer, k, 0)),
        ]
    else:
        tm, tf = FFN_TOKEN_TILE, COL_TILE
    final_norm = g_final is not None
    if not emit_bf16:
        any_spec = pl.BlockSpec(memory_space=pl.ANY)
        in_specs = [any_spec, pl.BlockSpec((1, d), lambda i: (0, 0)), any_spec, any_spec, any_spec]
        args = [x, g, wg, wu, wd]
        if final_norm:
            in_specs.append(pl.BlockSpec((1, d), lambda i: (0, 0)))
            args.append(g_final)
        out = pl.pallas_call(
            functools.partial(_ffn_loop_kernel, final_norm=final_norm),
            grid=(m // tm,),
            in_specs=in_specs,
            out_specs=pl.BlockSpec((tm, d), lambda i: (i, 0)),
            out_shape=jax.ShapeDtypeStruct((m, d), F32),
            scratch_shapes=[pltpu.VMEM((tm, d), BF16), pltpu.VMEM((tm, d), F32),
                            pltpu.VMEM((2, d, tf), BF16), pltpu.VMEM((2, d, tf), BF16),
                            pltpu.VMEM((2, tf, d), BF16),
                            pltpu.SemaphoreType.DMA(()), pltpu.SemaphoreType.DMA((3, 2))],
            compiler_params=_params(1),
            name="ffn_final" if final_norm else "ffn",
        )(*args)
        return out, ()
    in_specs = [pl.BlockSpec((tm, d), lambda i, k: (i, 0), pipeline_mode=pl.Buffered(1)),
                pl.BlockSpec((1, d), lambda i, k: (0, 0))] + w_specs
    args = [x, g, wg, wu, wd]
    if final_norm:
        in_specs.append(pl.BlockSpec((1, d), lambda i, k: (0, 0)))
        args.append(g_final)
    out_specs = [pl.BlockSpec((tm, d), lambda i, k: (i, 0)),
                 pl.BlockSpec((d, tf), lambda i, k: (0, k)),
                 pl.BlockSpec((d, tf), lambda i, k: (0, k)),
                 pl.BlockSpec((tf, d), lambda i, k: (k, 0))]
    out_shape = [jax.ShapeDtypeStruct((m, d), F32), jax.ShapeDtypeStruct((d, f), BF16),
                 jax.ShapeDtypeStruct((d, f), BF16), jax.ShapeDtypeStruct((f, d), BF16)]
    outs = pl.pallas_call(
        functools.partial(_ffn_kernel, final_norm=final_norm),
        grid=(m // tm, f // tf),
        in_specs=in_specs,
        out_specs=out_specs,
        out_shape=out_shape,
        scratch_shapes=[pltpu.VMEM((tm, d), BF16)],
        compiler_params=_params(2),
        name="ffn_cast" + ("_final" if final_norm else ""),
    )(*args)
    return outs[0], tuple(outs[1:])


def _pool_body(x_ref, hist, g_ref, pw_ref, sc_ref, o_ref, xe_ref, pos_first):
    bt, l, d = x_ref.shape
    grp = d // len(POOL_WINDOWS)
    x = x_ref[...]
    xe_ref[:, 0:POOL_HALO, :] = hist
    xe_ref[:, POOL_HALO:, :] = _rms(x, g_ref[...])
    pos = lax.broadcasted_iota(jnp.int32, (1, l, 1), 1) + pos_first
    for gi, w in enumerate(POOL_WINDOWS):
        c0 = gi * grp
        hg = xe_ref[:, POOL_HALO:, c0:c0 + grp]
        if bt == 1:
            a = xe_ref[0, :, c0:c0 + grp]
            span = 1
            while span < min(w, SUBLANES):
                a = a + pltpu.roll(a, span, axis=0)
                span *= 2
            acc = a[POOL_HALO:POOL_HALO + l]
            if w > SUBLANES:
                assert w == 2 * SUBLANES
                acc = acc + a[POOL_HALO - SUBLANES:POOL_HALO - SUBLANES + l]
            acc = acc[None]
        else:
            acc = hg
            for k in range(1, w):
                acc = acc + xe_ref[:, POOL_HALO - k:POOL_HALO - k + l, c0:c0 + grp]
        inv_cnt = 1.0 / jnp.minimum(pos + 1, w).astype(F32)
        diff = (acc * inv_cnt - hg).reshape(bt * l, grp).astype(BF16)
        y = _dot(diff, pw_ref[gi]) * sc_ref[:, c0:c0 + grp]
        o_ref[:, :, c0:c0 + grp] = x_ref[:, :, c0:c0 + grp] + y.reshape(bt, l, grp)


def _pool_prompt_kernel(x_ref, g_ref, pw_ref, sc_ref, o_ref, st_ref, xe_ref, carry_ref, *, pos0):
    i = pl.program_id(1)
    l = x_ref.shape[1]

    @pl.when(i == 0)
    def _():
        carry_ref[...] = jnp.zeros_like(carry_ref)

    _pool_body(x_ref, carry_ref[...], g_ref, pw_ref, sc_ref, o_ref, xe_ref, pos0 + i * l)
    carry_ref[...] = xe_ref[:, l:l + POOL_HALO, :]
    st_ref[...] = xe_ref[:, l + POOL_HALO - POOL_HIST:l + POOL_HALO, :]


def _pool_sample_kernel(x_ref, hist_ref, g_ref, pw_ref, sc_ref, o_ref, st_ref, xe_ref, *, pos0):
    l = x_ref.shape[1]
    _pool_body(x_ref, hist_ref[...], g_ref, pw_ref, sc_ref, o_ref, xe_ref, pos0)
    st_ref[...] = xe_ref[:, l + POOL_HALO - POOL_HIST:l + POOL_HALO, :]


def _pool_prompt(x, g, pw, scale, pos0):
    b, s, d = x.shape
    ts = TOKEN_TILE
    ng = len(POOL_WINDOWS)
    return pl.pallas_call(
        functools.partial(_pool_prompt_kernel, pos0=pos0),
        grid=(b, s // ts),
        in_specs=[
            pl.BlockSpec((1, ts, d), lambda bi, i: (bi, i, 0)),
            pl.BlockSpec((1, d), lambda bi, i: (0, 0)),
            pl.BlockSpec((ng, d // ng, d // ng), lambda bi, i: (0, 0, 0)),
            pl.BlockSpec((1, d), lambda bi, i: (0, 0)),
        ],
        out_specs=[
            pl.BlockSpec((1, ts, d), lambda bi, i: (bi, i, 0)),
            pl.BlockSpec((1, POOL_HIST, d), lambda bi, i: (bi, 0, 0)),
        ],
        out_shape=[jax.ShapeDtypeStruct((b, s, d), F32),
                   jax.ShapeDtypeStruct((b, POOL_HIST, d), F32)],
        scratch_shapes=[pltpu.VMEM((1, ts + POOL_HALO, d), F32),
                        pltpu.VMEM((1, POOL_HALO, d), F32)],
        compiler_params=_params(2),
        name="pool_prompt",
    )(x, g, pw, scale)


def _pool_sample(x, hist16, g, pw, scale, pos0):
    b, s, d = x.shape
    bt = 8
    ng = len(POOL_WINDOWS)
    return pl.pallas_call(
        functools.partial(_pool_sample_kernel, pos0=pos0),
        grid=(b // bt,),
        in_specs=[
            pl.BlockSpec((bt, s, d), lambda i: (i, 0, 0)),
            pl.BlockSpec((bt, POOL_HALO, d), lambda i: (i, 0, 0)),
            pl.BlockSpec((1, d), lambda i: (0, 0)),
            pl.BlockSpec((ng, d // ng, d // ng), lambda i: (0, 0, 0)),
            pl.BlockSpec((1, d), lambda i: (0, 0)),
        ],
        out_specs=[
            pl.BlockSpec((bt, s, d), lambda i: (i, 0, 0)),
            pl.BlockSpec((bt, POOL_HIST, d), lambda i: (i, 0, 0)),
        ],
        out_shape=[jax.ShapeDtypeStruct((b, s, d), F32),
                   jax.ShapeDtypeStruct((b, POOL_HIST, d), F32)],
        scratch_shapes=[pltpu.VMEM((bt, s + POOL_HALO, d), F32)],
        compiler_params=_params(1),
        name="pool_sample",
    )(x, hist16, g, pw, scale)


def _gmlp_kernel(x_ref, g_ref, win_ref, bin_ref, lng_ref, lnb_ref, ws_ref, bs_ref, wout_ref, o_ref, *rest,
                 block_diag, emit_v):
    if emit_v:
        vout_ref, z_ref, t_ref = rest
    else:
        z_ref, t_ref = rest
    tm = x_ref.shape[0]
    n_in, _, tn = win_ref.shape
    n_out, _, tno = wout_ref.shape
    width = n_in * tn // 2
    hdim = width // GMLP_HEADS

    h = _rms(x_ref[...], g_ref[...]).astype(BF16)
    for n in list(range(n_in // 2, n_in)) + list(range(n_in // 2)):
        cols = slice(n * tn, (n + 1) * tn)
        z_ref[:, cols] = jax.nn.gelu(_dot(h, win_ref[n]) + bin_ref[:, cols])
    v = _layernorm(z_ref[:, width:], lng_ref[...], lnb_ref[...])
    if emit_v:
        vout_ref[...] = v
    v = v.astype(BF16)

    row = lax.broadcasted_iota(jnp.int32, (GMLP_CHUNK, GMLP_CHUNK), 0)
    col = lax.broadcasted_iota(jnp.int32, (GMLP_CHUNK, GMLP_CHUNK), 1)
    if block_diag:
        shift = block_diag.bit_length() - 1
        mask = (row >> shift) == (col >> shift)
    else:
        mask = (col >> GMLP_CAUSAL_SHIFT) <= (row >> GMLP_CAUSAL_SHIFT)
    for head in range(GMLP_HEADS):
        lanes = slice(head * hdim, (head + 1) * hdim)
        wm = jnp.where(mask, ws_ref[head], 0.0).astype(BF16)
        bias = bs_ref[head]
        for c in range(tm // GMLP_CHUNK):
            rows = slice(c * GMLP_CHUNK, (c + 1) * GMLP_CHUNK)
            s = _dot(wm, v[rows, lanes]) + bias
            t_ref[rows, lanes] = (z_ref[rows, lanes] * s).astype(BF16)
    t = t_ref[...]
    for n in range(n_out):
        cols = slice(n * tno, (n + 1) * tno)
        o_ref[:, cols] = x_ref[:, cols] + _dot(t, wout_ref[n])


def _gmlp(x, g, w_in, b_in, ln_g, ln_b, ws, bs, w_out, block_diag, emit_v):
    m, d = x.shape
    n_in, _, tn = w_in.shape
    width = n_in * tn // 2
    tm = GMLP_TOKEN_TILE

    def resident(shape):
        return pl.BlockSpec(shape, lambda i: (0,) * len(shape), pipeline_mode=pl.Buffered(1))

    out_specs = [pl.BlockSpec((tm, d), lambda i: (i, 0))]
    out_shape = [jax.ShapeDtypeStruct((m, d), F32)]
    if emit_v:
        out_specs.append(pl.BlockSpec((tm, width), lambda i: (i, 0)))
        out_shape.append(jax.ShapeDtypeStruct((m, width), F32))
    outs = pl.pallas_call(
        functools.partial(_gmlp_kernel, block_diag=block_diag, emit_v=emit_v),
        grid=(m // tm,),
        in_specs=[
            pl.BlockSpec((tm, d), lambda i: (i, 0)),
            resident((1, d)),
            resident(w_in.shape),
            resident((1, 2 * width)),
            resident((1, width)),
            resident((1, width)),
            resident((GMLP_HEADS, GMLP_CHUNK, GMLP_CHUNK)),
            resident((GMLP_HEADS, GMLP_CHUNK, 1)),
            resident(w_out.shape),
        ],
        out_specs=out_specs,
        out_shape=out_shape,
        scratch_shapes=[pltpu.VMEM((tm, 2 * width), F32),
                        pltpu.VMEM((tm, width), BF16)],
        compiler_params=_params(1),
        name="gmlp_v" if emit_v else "gmlp",
    )(x, g, w_in, b_in, ln_g, ln_b, ws, bs, w_out)
    return outs if emit_v else (outs[0], None)


def _merge_col_blocks(state):
    if state.ndim == 3:
        return state
    n_seq, nb, rows, tn = state.shape
    return state.transpose(0, 2, 1, 3).reshape(n_seq, rows, nb * tn)


def _sconv_kernel(x_ref, g_ref, win_ref, cw_ref, wout_ref, *rest, sample, tiles_per_seq):
    if sample:
        hist_ref, o_ref, st_ref, cxe_ref, gated_ref = rest
    else:
        o_ref, st_ref, cxe_ref, gated_ref, carry_ref = rest
    i = pl.program_id(0)
    tm = x_ref.shape[0]
    nb, _, tn = wout_ref.shape
    halo = SCONV_HALO
    first_tap = halo - (SCONV_WIDTH - 1)

    h = _rms(x_ref[...], g_ref[...]).astype(BF16)

    if not sample:
        @pl.when(i % tiles_per_seq == 0)
        def _():
            carry_ref[...] = jnp.zeros_like(carry_ref)

    for n in range(nb):
        cols = slice(n * tn, (n + 1) * tn)
        b_gate = _dot(h, win_ref[n])
        cx = _dot(h, win_ref[nb + n]) * _dot(h, win_ref[2 * nb + n])
        if sample:
            bt, l = cxe_ref.shape[0], cxe_ref.shape[1] - halo
            cxe_ref[:, 0:halo, :] = hist_ref[:, :, cols]
            cxe_ref[:, halo:, :] = cx.reshape(bt, l, tn)
            conv = None
            for k in range(SCONV_WIDTH):
                term = cw_ref[k:k + 1, cols] * cxe_ref[:, first_tap + k:first_tap + k + l, :]
                conv = term if conv is None else conv + term
            conv = conv.reshape(tm, tn)
            st_ref[:, :, cols] = cxe_ref[:, l + first_tap:l + halo, :]
        else:
            cxe_ref[n, 0:halo, :] = carry_ref[n]
            cxe_ref[n, halo:, :] = cx
            carry_ref[n] = cxe_ref[n, tm:tm + halo, :]
            conv = None
            for k in range(SCONV_WIDTH):
                term = cw_ref[k:k + 1, cols] * cxe_ref[n, first_tap + k:first_tap + k + tm, :]
                conv = term if conv is None else conv + term
            st_ref[i // tiles_per_seq, n] = cxe_ref[n, tm + first_tap:tm + halo, :]
        gated_ref[:, cols] = (b_gate * conv).astype(BF16)

    gated = gated_ref[...]
    for n in range(nb):
        cols = slice(n * tn, (n + 1) * tn)
        o_ref[:, cols] = x_ref[:, cols] + _dot(gated, wout_ref[n])


def _sconv(x, g, w_in, conv_w, w_out, hist8, n_seq):
    m, d = x.shape
    sample = hist8 is not None
    nb, _, tn = w_out.shape
    hist_rows = SCONV_WIDTH - 1
    tm = SCONV_TOKEN_TILE
    l = m // n_seq
    tiles_per_seq = None if sample else l // tm
    bt = tm // l if sample else None

    def resident(shape):
        return pl.BlockSpec(shape, lambda i: (0,) * len(shape), pipeline_mode=pl.Buffered(1))

    in_specs = [
        pl.BlockSpec((tm, d), lambda i: (i, 0)),
        resident((1, d)),
        resident(w_in.shape),
        resident((SCONV_WIDTH, d)),
        resident(w_out.shape),
    ]
    args = [x, g, w_in, conv_w, w_out]
    if sample:
        in_specs.append(pl.BlockSpec((bt, SCONV_HALO, d), lambda i: (i, 0, 0)))
        args.append(hist8)
        st_spec = pl.BlockSpec((bt, hist_rows, d), lambda i: (i, 0, 0))
        scratch = [pltpu.VMEM((bt, l + SCONV_HALO, tn), F32), pltpu.VMEM((tm, d), BF16)]
    else:
        st_spec = pl.BlockSpec((n_seq, nb, hist_rows, tn), lambda i: (0, 0, 0, 0))
        scratch = [pltpu.VMEM((nb, tm + SCONV_HALO, tn), F32), pltpu.VMEM((tm, d), BF16),
                   pltpu.VMEM((nb, SCONV_HALO, tn), F32)]
    st_shape = (n_seq, hist_rows, d) if sample else (n_seq, nb, hist_rows, tn)
    out, state = pl.pallas_call(
        functools.partial(_sconv_kernel, sample=sample, tiles_per_seq=tiles_per_seq),
        grid=(m // tm,),
        in_specs=in_specs,
        out_specs=[pl.BlockSpec((tm, d), lambda i: (i, 0)), st_spec],
        out_shape=[jax.ShapeDtypeStruct((m, d), F32),
                   jax.ShapeDtypeStruct(st_shape, F32)],
        scratch_shapes=scratch,
        compiler_params=_params(1),
        name="sconv_sample" if sample else "sconv_prompt",
    )(*args)
    return out, _merge_col_blocks(state)


def _causal_taps_flat(glue_ref, n, w_ref, c0, tn, tm, halo, width):
    first_tap = halo - (width - 1)
    ext = tm + SUBLANES
    conv = None
    for r in range(SUBLANES):
        part = None
        for o in range(first_tap, first_tap + width):
            if o % SUBLANES != r:
                continue
            k = o - first_tap
            rows = tm if r == 0 else ext
            term = w_ref[k:k + 1, c0:c0 + tn] * glue_ref[n, o - r:o - r + rows, :]
            part = term if part is None else part + term
        if part is None:
            continue
        if r:
            part = pltpu.roll(part, ext - r, axis=0)[:tm]
        conv = part if conv is None else conv + part
    return conv


def _cconv_kernel(x_ref, g_ref, w1_ref, b1_ref, dw_ref, dwb_ref, lng_ref, lnb_ref, w2_ref, b2_ref,
                  *rest, sample, tiles_per_seq, nb):
    if sample:
        hist_ref, o_ref, st_ref, glue_ref, conv_ref = rest
    else:
        o_ref, st_ref, glue_ref, conv_ref, carry_ref = rest
    i = pl.program_id(0)
    tm, d = x_ref.shape
    tn = d // nb
    halo = CCONV_HALO
    first_tap = halo - (CCONV_WIDTH - 1)

    x = x_ref[...]
    h = _rms(x, g_ref[...]).astype(BF16)

    if not sample:
        @pl.when(i % tiles_per_seq == 0)
        def _():
            carry_ref[...] = jnp.zeros_like(carry_ref)

    for n in range(nb):
        c0 = n * tn
        a = _dot(h, w1_ref[n]) + b1_ref[:, c0:c0 + tn]
        gt = _dot(h, w1_ref[nb + n]) + b1_ref[:, d + c0:d + c0 + tn]
        glu = a * jax.nn.sigmoid(gt)
        if sample:
            bt, l = glue_ref.shape[0], glue_ref.shape[1] - halo
            glue_ref[:, 0:halo, :] = hist_ref[:, :, c0:c0 + tn]
            glue_ref[:, halo:, :] = glu.reshape(bt, l, tn)
            conv = None
            for k in range(CCONV_WIDTH):
                term = dw_ref[k:k + 1, c0:c0 + tn] * glue_ref[:, first_tap + k:first_tap + k + l, :]
                conv = term if conv is None else conv + term
            conv = conv.reshape(tm, tn)
            st_ref[:, :, c0:c0 + tn] = glue_ref[:, l + first_tap:l + halo, :]
        else:
            glue_ref[n, 0:halo, :] = carry_ref[n]
            glue_ref[n, halo:, :] = glu
            carry_ref[n] = glue_ref[n, tm:tm + halo, :]
            conv = _causal_taps_flat(glue_ref, n, dw_ref, c0, tn, tm, halo, CCONV_WIDTH)
            st_ref[i // tiles_per_seq, n] = glue_ref[n, tm + first_tap:tm + halo, :]
        conv_ref[:, c0:c0 + tn] = conv + dwb_ref[:, c0:c0 + tn]

    y = _layernorm(conv_ref[...], lng_ref[...], lnb_ref[...])
    z = (y * jax.nn.sigmoid(y)).astype(BF16)
    for n in range(nb):
        cols = slice(n * tn, (n + 1) * tn)
        o_ref[:, cols] = x_ref[:, cols] + _dot(z, w2_ref[n]) + b2_ref[:, cols]


def _cast_kernel(w_ref, o_ref):
    o_ref[...] = w_ref[...].astype(o_ref.dtype)


def _col_blocks_bf16(w, tn):
    k, n = w.shape
    return pl.pallas_call(
        _cast_kernel,
        grid=(n // tn,),
        in_specs=[pl.BlockSpec((k, tn), lambda j: (0, j))],
        out_specs=pl.BlockSpec((None, k, tn), lambda j: (j, 0, 0)),
        out_shape=jax.ShapeDtypeStruct((n // tn, k, tn), BF16),
        compiler_params=_params(1),
        name="cast_col_blocks",
    )(w)


def _cconv(x, g, w1, b1, dw_w, dw_b, ln_g, ln_b, w2, b2, hist32, n_seq):
    m, d = x.shape
    sample = hist32 is not None
    nb, _, tn = w2.shape
    hist_rows = CCONV_WIDTH - 1
    tm = CCONV_TOKEN_TILE
    l = m // n_seq
    tiles_per_seq = None if sample else l // tm
    bt = tm // l if sample else None

    def resident(shape):
        return pl.BlockSpec(shape, lambda i: (0,) * len(shape), pipeline_mode=pl.Buffered(1))

    in_specs = [
        pl.BlockSpec((tm, d), lambda i: (i, 0)),
        resident((1, d)),
        resident((2 * nb, d, tn)),
        resident((1, 2 * d)),
        resident((CCONV_WIDTH, d)),
        resident((1, d)),
        resident((1, d)),
        resident((1, d)),
        resident((nb, d, tn)),
        resident((1, d)),
    ]
    args = [x, g, w1, b1, dw_w, dw_b, ln_g, ln_b, w2, b2]
    if sample:
        in_specs.append(pl.BlockSpec((bt, CCONV_HALO, d), lambda i: (i, 0, 0), pipeline_mode=pl.Buffered(1)))
        args.append(hist32)
        st_spec = pl.BlockSpec((bt, hist_rows, d), lambda i: (i, 0, 0))
        scratch = [pltpu.VMEM((bt, l + CCONV_HALO, tn), F32), pltpu.VMEM((tm, d), F32)]
    else:
        st_spec = pl.BlockSpec((n_seq, nb, hist_rows, tn), lambda i: (0, 0, 0, 0))
        scratch = [pltpu.VMEM((nb, tm + CCONV_HALO, tn), F32), pltpu.VMEM((tm, d), F32),
                   pltpu.VMEM((nb, CCONV_HALO, tn), F32)]
    st_shape = (n_seq, hist_rows, d) if sample else (n_seq, nb, hist_rows, tn)
    out, state = pl.pallas_call(
        functools.partial(_cconv_kernel, sample=sample, tiles_per_seq=tiles_per_seq, nb=nb),
        grid=(m // tm,),
        in_specs=in_specs,
        out_specs=[pl.BlockSpec((tm, d), lambda i: (i, 0)), st_spec],
        out_shape=[jax.ShapeDtypeStruct((m, d), F32),
                   jax.ShapeDtypeStruct(st_shape, F32)],
        scratch_shapes=scratch,
        compiler_params=_params(1),
        name="cconv_sample" if sample else "cconv_prompt",
    )(*args)
    return out, _merge_col_blocks(state)


def _pad_front(hist, rows):
    return jnp.pad(hist, ((0, 0), (rows - hist.shape[1], 0), (0, 0)))


def _trunk(x, pool_hist, sconv_hist, cconv_hist, pos0, p, ffn_weights):
    b, s, d = x.shape
    sample = pool_hist is not None
    row = lambda v: v.reshape(1, -1)

    if sample:
        x, pool_state = _pool_sample(x, _pad_front(pool_hist, POOL_HALO), row(p['norm_mix_g'][0]),
                                     p['pool_w'], row(p['pool_scale']), pos0)
    else:
        x, pool_state = _pool_prompt(x, row(p['norm_mix_g'][0]), p['pool_w'], row(p['pool_scale']), pos0)
    x = x.reshape(b * s, d)
    used_weights = []

    def ffn(x, layer, g_final=None):
        if ffn_weights is None:
            w = (p['ffn_w_gate'], p['ffn_w_up'], p['ffn_w_down'])
        else:
            w = ffn_weights[layer]
        out, w_bf16 = _ffn(x, row(p['norm_ffn_g'][layer]), *w, layer, g_final)
        used_weights.append(w_bf16 or w)
        return out

    x = ffn(x, 0)

    if sample:
        reps = GMLP_CHUNK // s
        ws = jnp.tile(p['gmlp_w_s'][:, :s, :s], (1, reps, reps))
        bs = jnp.tile(p['gmlp_b_s'][:, :s], (1, reps))
        block_diag = s
    else:
        ws, bs, block_diag = p['gmlp_w_s'], p['gmlp_b_s'], 0
    x, v = _gmlp(x, row(p['norm_mix_g'][1]), p['gmlp_w_in'], row(p['gmlp_b_in']), row(p['gmlp_ln_g']),
                 row(p['gmlp_ln_b']), ws, bs[:, :, None], p['gmlp_w_out'], block_diag, emit_v=sample)
    x = ffn(x, 1)

    hist8 = _pad_front(sconv_hist, SCONV_HALO) if sample else None
    x, sconv_state = _sconv(x, row(p['norm_mix_g'][2]), p['sconv_w_in'], p['sconv_conv_w'],
                            p['sconv_w_out'], hist8, b)
    x = ffn(x, 2)

    hist32 = _pad_front(cconv_hist, CCONV_HALO) if sample else None
    x, cconv_state = _cconv(x, row(p['norm_mix_g'][3]), p['cconv_w_pw1'], row(p['cconv_b_pw1']),
                            p['cconv_dw_w'], row(p['cconv_dw_b']), row(p['cconv_ln_g']),
                            row(p['cconv_ln_b']), p['cconv_w_pw2'], row(p['cconv_b_pw2']), hist32, b)
    y = ffn(x, 3, row(p['norm_final_g']))
    gmlp_v = v.reshape(b, s, -1) if sample else None
    return y.reshape(b, s, d), pool_state, gmlp_v, sconv_state, cconv_state, used_weights


MIXER_MATMUL_WEIGHTS = ('pool_w',)
COL_BLOCKED_WEIGHTS = ('gmlp_w_in', 'gmlp_w_out', 'sconv_w_in', 'sconv_w_out', 'cconv_w_pw1', 'cconv_w_pw2')


def kernel(x_prompt, x_sample, state_pool, state_sconv, state_cconv, norm_mix_g, norm_ffn_g, norm_final_g, pool_w, pool_scale, gmlp_w_in, gmlp_b_in, gmlp_ln_g, gmlp_ln_b, gmlp_w_s, gmlp_b_s, gmlp_w_out, sconv_w_in, sconv_conv_w, sconv_w_out, cconv_w_pw1, cconv_b_pw1, cconv_dw_w, cconv_dw_b, cconv_ln_g, cconv_ln_b, cconv_w_pw2, cconv_b_pw2, ffn_w_gate, ffn_w_up, ffn_w_down):
    p = dict(norm_mix_g=norm_mix_g, norm_ffn_g=norm_ffn_g, norm_final_g=norm_final_g,
             pool_w=pool_w, pool_scale=pool_scale,
             gmlp_w_in=gmlp_w_in, gmlp_b_in=gmlp_b_in, gmlp_ln_g=gmlp_ln_g, gmlp_ln_b=gmlp_ln_b,
             gmlp_w_s=gmlp_w_s, gmlp_b_s=gmlp_b_s, gmlp_w_out=gmlp_w_out,
             sconv_w_in=sconv_w_in, sconv_conv_w=sconv_conv_w, sconv_w_out=sconv_w_out,
             cconv_w_pw1=cconv_w_pw1, cconv_b_pw1=cconv_b_pw1, cconv_dw_w=cconv_dw_w, cconv_dw_b=cconv_dw_b,
             cconv_ln_g=cconv_ln_g, cconv_ln_b=cconv_ln_b, cconv_w_pw2=cconv_w_pw2, cconv_b_pw2=cconv_b_pw2,
             ffn_w_gate=ffn_w_gate, ffn_w_up=ffn_w_up, ffn_w_down=ffn_w_down)
    for name in MIXER_MATMUL_WEIGHTS:
        p[name] = p[name].astype(BF16)
    for name in COL_BLOCKED_WEIGHTS:
        p[name] = _col_blocks_bf16(p[name], COL_TILE)
    y_s, pool_s, gmlp_v_s, sconv_s, cconv_s, ffn_weights = _trunk(
        x_sample, state_pool, state_sconv, state_cconv, PAST_LEN, p, None)
    y_p, pool_p, _, sconv_p, cconv_p, _ = _trunk(x_prompt, None, None, None, 0, p, ffn_weights)
    return (y_p, y_s, pool_p, pool_s, gmlp_v_s, sconv_p, sconv_s, cconv_p, cconv_s)
```

```python
import functools

import jax
import jax.numpy as jnp
from jax import lax
from jax.experimental import pallas as pl
from jax.experimental.pallas import tpu as pltpu

F32 = jnp.float32
BF16 = jnp.bfloat16

EPS = 1e-6
PAST_LEN = 4096
POOL_WINDOWS = (2, 4, 8, 16)
POOL_HIST = max(POOL_WINDOWS) - 1
POOL_HALO = 16
GMLP_CHUNK = 128
GMLP_CAUSAL_SHIFT = 6
GMLP_HEADS = 8
GMLP_TOKEN_TILE = 256
SCONV_TOKEN_TILE = 256
SCONV_WIDTH = 3
SCONV_HALO = 8
CCONV_WIDTH = 31
CCONV_HALO = 32

SUBLANES = 8
TOKEN_TILE = 512
CCONV_TOKEN_TILE = 256
FFN_TOKEN_TILE = 1024
FFN_CAST_TOKEN_TILE = 512
FFN_CAST_COL_TILE = 512
COL_TILE = 512
VMEM_LIMIT_BYTES = 56 * 1024 * 1024


def _rms(x, g):
    ms = jnp.mean(x * x, axis=-1, keepdims=True)
    return x * lax.rsqrt(ms + EPS) * g


def _layernorm(x, g, b):
    mu = jnp.mean(x, axis=-1, keepdims=True)
    xc = x - mu
    var = jnp.mean(xc * xc, axis=-1, keepdims=True)
    return xc * lax.rsqrt(var + EPS) * g + b


def _dot(a, b):
    return jnp.dot(a, b, preferred_element_type=F32)


def _params(n_axes):
    return pltpu.CompilerParams(
        dimension_semantics=("arbitrary",) * n_axes,
        vmem_limit_bytes=VMEM_LIMIT_BYTES)


def _ffn_kernel(x_ref, g_ref, wg_ref, wu_ref, wd_ref, *rest, final_norm):
    rest = list(rest)
    gf_ref = rest.pop(0) if final_norm else None
    o_ref, wg_out_ref, wu_out_ref, wd_out_ref, h_ref = rest
    k = pl.program_id(1)

    @pl.when(k == 0)
    def _():
        x = x_ref[...]
        h_ref[...] = _rms(x, g_ref[...]).astype(BF16)
        o_ref[...] = x

    wg, wu, wd = wg_ref[...].astype(BF16), wu_ref[...].astype(BF16), wd_ref[...].astype(BF16)
    wg_out_ref[...] = wg
    wu_out_ref[...] = wu
    wd_out_ref[...] = wd
    h = h_ref[...]
    gate = _dot(h, wg)
    up = _dot(h, wu)
    act = (gate * jax.nn.sigmoid(gate) * up).astype(BF16)
    o_ref[...] += _dot(act, wd)

    if final_norm:
        @pl.when(k == pl.num_programs(1) - 1)
        def _():
            o_ref[...] = _rms(o_ref[...], gf_ref[...])


def _ffn_loop_kernel(x_hbm, g_ref, wg_hbm, wu_hbm, wd_hbm, *rest, final_norm):
    rest = list(rest)
    gf_ref = rest.pop(0) if final_norm else None
    o_ref, h_ref, x_tile, wg_buf, wu_buf, wd_buf, x_sem, w_sem = rest
    i = pl.program_id(0)
    n_tiles = pl.num_programs(0)
    tm = x_tile.shape[0]
    tf = wg_buf.shape[2]
    nk = wg_hbm.shape[1] // tf

    def fetch_x(tile):
        return pltpu.make_async_copy(x_hbm.at[pl.ds(tile * tm, tm)], x_tile, x_sem)

    def fetch_w(k, slot):
        cols = pl.ds(pl.multiple_of(k * tf, tf), tf)
        return (pltpu.make_async_copy(wg_hbm.at[:, cols], wg_buf.at[slot], w_sem.at[0, slot]),
                pltpu.make_async_copy(wu_hbm.at[:, cols], wu_buf.at[slot], w_sem.at[1, slot]),
                pltpu.make_async_copy(wd_hbm.at[cols, :], wd_buf.at[slot], w_sem.at[2, slot]))

    first_slot = (i * nk) % 2

    @pl.when(i == 0)
    def _():
        fetch_x(0).start()
        for c in fetch_w(0, 0):
            c.start()

    def advance(k):
        slot = (first_slot + k) % 2
        for c in fetch_w(k, slot):
            c.wait()
        more_here = k + 1 < nk

        @pl.when(more_here | (i + 1 < n_tiles))
        def _():
            for c in fetch_w(jnp.where(more_here, k + 1, 0), 1 - slot):
                c.start()

        return slot

    def hidden_block(h, slot):
        half = tf // 2
        acc = None
        for c in range(2):
            cols = slice(c * half, (c + 1) * half)
            gate = _dot(h, wg_buf[slot, :, cols])
            up = _dot(h, wu_buf[slot, :, cols])
            act = (gate * jax.nn.sigmoid(gate) * up).astype(BF16)
            part = _dot(act, wd_buf[slot, cols, :])
            acc = part if acc is None else acc + part
        return acc

    fetch_x(i).wait()
    slot0 = advance(0)
    for r in range(2):
        rows = slice(r * (tm // 2), (r + 1) * (tm // 2))
        h = _rms(x_tile[rows, :], g_ref[...]).astype(BF16)
        h_ref[rows, :] = h
        o_ref[rows, :] = x_tile[rows, :] + hidden_block(h, slot0)

    def step(k, carry):
        slot = advance(k)

        @pl.when((k == 1) & (i + 1 < n_tiles))
        def _():
            fetch_x(i + 1).start()

        o_ref[...] += hidden_block(h_ref[...], slot)
        return carry

    lax.fori_loop(1, nk, step, 0)

    if final_norm:
        o_ref[...] = _rms(o_ref[...], gf_ref[...])


def _ffn(x, g, wg, wu, wd, layer, g_final=None):
    m, d = x.shape
    emit_bf16 = wg.ndim == 3
    f = wg.shape[-1]
    if emit_bf16:
        tm, tf = FFN_CAST_TOKEN_TILE, FFN_CAST_COL_TILE
        assert m == tm, "every weight block must be visited exactly once"
        w_specs = [
            pl.BlockSpec((None, d, tf), lambda i, k: (layer, 0, k)),
            pl.BlockSpec((None, d, tf), lambda i, k: (layer, 0, k)),
            pl.BlockSpec((None, tf, d), lambda i, k: (layer, k, 0)),
        ]
    else:
        tm, tf = FFN_TOKEN_TILE, COL_TILE
    final_norm = g_final is not None
    if not emit_bf16:
        any_spec = pl.BlockSpec(memory_space=pl.ANY)
        in_specs = [any_spec, pl.BlockSpec((1, d), lambda i: (0, 0)), any_spec, any_spec, any_spec]
        args = [x, g, wg, wu, wd]
        if final_norm:
            in_specs.append(pl.BlockSpec((1, d), lambda i: (0, 0)))
            args.append(g_final)
        out = pl.pallas_call(
            functools.partial(_ffn_loop_kernel, final_norm=final_norm),
            grid=(m // tm,),
            in_specs=in_specs,
            out_specs=pl.BlockSpec((tm, d), lambda i: (i, 0)),
            out_shape=jax.ShapeDtypeStruct((m, d), F32),
            scratch_shapes=[pltpu.VMEM((tm, d), BF16), pltpu.VMEM((tm, d), F32),
                            pltpu.VMEM((2, d, tf), BF16), pltpu.VMEM((2, d, tf), BF16),
                            pltpu.VMEM((2, tf, d), BF16),
                            pltpu.SemaphoreType.DMA(()), pltpu.SemaphoreType.DMA((3, 2))],
            compiler_params=_params(1),
            name="ffn_final" if final_norm else "ffn",
        )(*args)
        return out, ()
    in_specs = [pl.BlockSpec((tm, d), lambda i, k: (i, 0), pipeline_mode=pl.Buffered(1)),
                pl.BlockSpec((1, d), lambda i, k: (0, 0))] + w_specs
    args = [x, g, wg, wu, wd]
    if final_norm:
        in_specs.append(pl.BlockSpec((1, d), lambda i, k: (0, 0)))
        args.append(g_final)
    out_specs = [pl.BlockSpec((tm, d), lambda i, k: (i, 0)),
                 pl.BlockSpec((d, tf), lambda i, k: (0, k)),
                 pl.BlockSpec((d, tf), lambda i, k: (0, k)),
                 pl.BlockSpec((tf, d), lambda i, k: (k, 0))]
    out_shape = [jax.ShapeDtypeStruct((m, d), F32), jax.ShapeDtypeStruct((d, f), BF16),
                 jax.ShapeDtypeStruct((d, f), BF16), jax.ShapeDtypeStruct((f, d), BF16)]
    outs = pl.pallas_call(
        functools.partial(_ffn_kernel, final_norm=final_norm),
        grid=(m // tm, f // tf),
        in_specs=in_specs,
        out_specs=out_specs,
        out_shape=out_shape,
        scratch_shapes=[pltpu.VMEM((tm, d), BF16)],
        compiler_params=_params(2),
        name="ffn_cast" + ("_final" if final_norm else ""),
    )(*args)
    return outs[0], tuple(outs[1:])


def _pool_body(x_ref, hist, g_ref, pw_ref, sc_ref, o_ref, xe_ref, pos_first):
    bt, l, d = x_ref.shape
    grp = d // len(POOL_WINDOWS)
    x = x_ref[...]
    xe_ref[:, 0:POOL_HALO, :] = hist
    xe_ref[:, POOL_HALO:, :] = _rms(x, g_ref[...])
    pos = lax.broadcasted_iota(jnp.int32, (1, l, 1), 1) + pos_first
    for gi, w in enumerate(POOL_WINDOWS):
        c0 = gi * grp
        hg = xe_ref[:, POOL_HALO:, c0:c0 + grp]
        if bt == 1:
            a = xe_ref[0, :, c0:c0 + grp]
            span = 1
            while span < min(w, SUBLANES):
                a = a + pltpu.roll(a, span, axis=0)
                span *= 2
            acc = a[POOL_HALO:POOL_HALO + l]
            if w > SUBLANES:
                assert w == 2 * SUBLANES
                acc = acc + a[POOL_HALO - SUBLANES:POOL_HALO - SUBLANES + l]
            acc = acc[None]
        else:
            acc = hg
            for k in range(1, w):
                acc = acc + xe_ref[:, POOL_HALO - k:POOL_HALO - k + l, c0:c0 + grp]
        inv_cnt = 1.0 / jnp.minimum(pos + 1, w).astype(F32)
        diff = (acc * inv_cnt - hg).reshape(bt * l, grp).astype(BF16)
        y = _dot(diff, pw_ref[gi]) * sc_ref[:, c0:c0 + grp]
        o_ref[:, :, c0:c0 + grp] = x_ref[:, :, c0:c0 + grp] + y.reshape(bt, l, grp)


def _pool_prompt_kernel(x_ref, g_ref, pw_ref, sc_ref, o_ref, st_ref, xe_ref, carry_ref, *, pos0):
    i = pl.program_id(1)
    l = x_ref.shape[1]

    @pl.when(i == 0)
    def _():
        carry_ref[...] = jnp.zeros_like(carry_ref)

    _pool_body(x_ref, carry_ref[...], g_ref, pw_ref, sc_ref, o_ref, xe_ref, pos0 + i * l)
    carry_ref[...] = xe_ref[:, l:l + POOL_HALO, :]
    st_ref[...] = xe_ref[:, l + POOL_HALO - POOL_HIST:l + POOL_HALO, :]


def _pool_sample_kernel(x_ref, hist_ref, g_ref, pw_ref, sc_ref, o_ref, st_ref, xe_ref, *, pos0):
    l = x_ref.shape[1]
    _pool_body(x_ref, hist_ref[...], g_ref, pw_ref, sc_ref, o_ref, xe_ref, pos0)
    st_ref[...] = xe_ref[:, l + POOL_HALO - POOL_HIST:l + POOL_HALO, :]


def _pool_prompt(x, g, pw, scale, pos0):
    b, s, d = x.shape
    ts = TOKEN_TILE
    ng = len(POOL_WINDOWS)
    return pl.pallas_call(
        functools.partial(_pool_prompt_kernel, pos0=pos0),
        grid=(b, s // ts),
        in_specs=[
            pl.BlockSpec((1, ts, d), lambda bi, i: (bi, i, 0)),
            pl.BlockSpec((1, d), lambda bi, i: (0, 0)),
            pl.BlockSpec((ng, d // ng, d // ng), lambda bi, i: (0, 0, 0)),
            pl.BlockSpec((1, d), lambda bi, i: (0, 0)),
        ],
        out_specs=[
            pl.BlockSpec((1, ts, d), lambda bi, i: (bi, i, 0)),
            pl.BlockSpec((1, POOL_HIST, d), lambda bi, i: (bi, 0, 0)),
        ],
        out_shape=[jax.ShapeDtypeStruct((b, s, d), F32),
                   jax.ShapeDtypeStruct((b, POOL_HIST, d), F32)],
        scratch_shapes=[pltpu.VMEM((1, ts + POOL_HALO, d), F32),
                        pltpu.VMEM((1, POOL_HALO, d), F32)],
        compiler_params=_params(2),
        name="pool_prompt",
    )(x, g, pw, scale)


def _pool_sample(x, hist16, g, pw, scale, pos0):
    b, s, d = x.shape
    bt = 8
    ng = len(POOL_WINDOWS)
    return pl.pallas_call(
        functools.partial(_pool_sample_kernel, pos0=pos0),
        grid=(b // bt,),
        in_specs=[
            pl.BlockSpec((bt, s, d), lambda i: (i, 0, 0)),
            pl.BlockSpec((bt, POOL_HALO, d), lambda i: (i, 0, 0)),
            pl.BlockSpec((1, d), lambda i: (0, 0)),
            pl.BlockSpec((ng, d // ng, d // ng), lambda i: (0, 0, 0)),
            pl.BlockSpec((1, d), lambda i: (0, 0)),
        ],
        out_specs=[
            pl.BlockSpec((bt, s, d), lambda i: (i, 0, 0)),
            pl.BlockSpec((bt, POOL_HIST, d), lambda i: (i, 0, 0)),
        ],
        out_shape=[jax.ShapeDtypeStruct((b, s, d), F32),
                   jax.ShapeDtypeStruct((b, POOL_HIST, d), F32)],
        scratch_shapes=[pltpu.VMEM((bt, s + POOL_HALO, d), F32)],
        compiler_params=_params(1),
        name="pool_sample",
    )(x, hist16, g, pw, scale)


def _gmlp_kernel(x_ref, g_ref, win_ref, bin_ref, lng_ref, lnb_ref, ws_ref, bs_ref, wout_ref, o_ref, *rest,
                 block_diag, emit_v):
    if emit_v:
        vout_ref, z_ref, t_ref = rest
    else:
        z_ref, t_ref = rest
    tm = x_ref.shape[0]
    n_in, _, tn = win_ref.shape
    n_out, _, tno = wout_ref.shape
    width = n_in * tn // 2
    hdim = width // GMLP_HEADS

    h = _rms(x_ref[...], g_ref[...]).astype(BF16)

    def project(n):
        cols = slice(n * tn, (n + 1) * tn)
        z_ref[:, cols] = jax.nn.gelu(_dot(h, win_ref[n]) + bin_ref[:, cols])

    for n in range(n_in // 2, n_in):
        project(n)
    v = _layernorm(z_ref[:, width:], lng_ref[...], lnb_ref[...])
    if emit_v:
        vout_ref[...] = v
    v = v.astype(BF16)

    row = lax.broadcasted_iota(jnp.int32, (GMLP_CHUNK, GMLP_CHUNK), 0)
    col = lax.broadcasted_iota(jnp.int32, (GMLP_CHUNK, GMLP_CHUNK), 1)
    if block_diag:
        shift = block_diag.bit_length() - 1
        mask = (row >> shift) == (col >> shift)
    else:
        mask = (col >> GMLP_CAUSAL_SHIFT) <= (row >> GMLP_CAUSAL_SHIFT)
    heads_per_block = tn // hdim
    for n in range(n_in // 2):
        project(n)
        for head in range(n * heads_per_block, (n + 1) * heads_per_block):
            lanes = slice(head * hdim, (head + 1) * hdim)
            wm = jnp.where(mask, ws_ref[head], 0.0).astype(BF16)
            bias = bs_ref[head]
            for c in range(tm // GMLP_CHUNK):
                rows = slice(c * GMLP_CHUNK, (c + 1) * GMLP_CHUNK)
                s = _dot(wm, v[rows, lanes]) + bias
                t_ref[rows, lanes] = (z_ref[rows, lanes] * s).astype(BF16)
    t = t_ref[...]
    for n in range(n_out):
        cols = slice(n * tno, (n + 1) * tno)
        o_ref[:, cols] = x_ref[:, cols] + _dot(t, wout_ref[n])


def _gmlp(x, g, w_in, b_in, ln_g, ln_b, ws, bs, w_out, block_diag, emit_v):
    m, d = x.shape
    n_in, _, tn = w_in.shape
    width = n_in * tn // 2
    tm = GMLP_TOKEN_TILE

    def resident(shape):
        return pl.BlockSpec(shape, lambda i: (0,) * len(shape), pipeline_mode=pl.Buffered(1))

    out_specs = [pl.BlockSpec((tm, d), lambda i: (i, 0))]
    out_shape = [jax.ShapeDtypeStruct((m, d), F32)]
    if emit_v:
        out_specs.append(pl.BlockSpec((tm, width), lambda i: (i, 0)))
        out_shape.append(jax.ShapeDtypeStruct((m, width), F32))
    outs = pl.pallas_call(
        functools.partial(_gmlp_kernel, block_diag=block_diag, emit_v=emit_v),
        grid=(m // tm,),
        in_specs=[
            pl.BlockSpec((tm, d), lambda i: (i, 0)),
            resident((1, d)),
            resident(w_in.shape),
            resident((1, 2 * width)),
            resident((1, width)),
            resident((1, width)),
            resident((GMLP_HEADS, GMLP_CHUNK, GMLP_CHUNK)),
            resident((GMLP_HEADS, GMLP_CHUNK, 1)),
            resident(w_out.shape),
        ],
        out_specs=out_specs,
        out_shape=out_shape,
        scratch_shapes=[pltpu.VMEM((tm, 2 * width), F32),
                        pltpu.VMEM((tm, width), BF16)],
        compiler_params=_params(1),
        name="gmlp_v" if emit_v else "gmlp",
    )(x, g, w_in, b_in, ln_g, ln_b, ws, bs, w_out)
    return outs if emit_v else (outs[0], None)


def _merge_col_blocks(state):
    if state.ndim == 3:
        return state
    n_seq, nb, rows, tn = state.shape
    return state.transpose(0, 2, 1, 3).reshape(n_seq, rows, nb * tn)


def _sconv_kernel(x_ref, g_ref, win_ref, cw_ref, wout_ref, *rest, sample, tiles_per_seq):
    if sample:
        hist_ref, o_ref, st_ref, cxe_ref, gated_ref = rest
    else:
        o_ref, st_ref, cxe_ref, gated_ref, carry_ref = rest
    i = pl.program_id(0)
    tm = x_ref.shape[0]
    nb, _, tn = wout_ref.shape
    halo = SCONV_HALO
    first_tap = halo - (SCONV_WIDTH - 1)

    h = _rms(x_ref[...], g_ref[...]).astype(BF16)

    if not sample:
        @pl.when(i % tiles_per_seq == 0)
        def _():
            carry_ref[...] = jnp.zeros_like(carry_ref)

    for n in range(nb):
        cols = slice(n * tn, (n + 1) * tn)
        b_gate = _dot(h, win_ref[n])
        cx = _dot(h, win_ref[nb + n]) * _dot(h, win_ref[2 * nb + n])
        if sample:
            bt, l = cxe_ref.shape[0], cxe_ref.shape[1] - halo
            cxe_ref[:, 0:halo, :] = hist_ref[:, :, cols]
            cxe_ref[:, halo:, :] = cx.reshape(bt, l, tn)
            conv = None
            for k in range(SCONV_WIDTH):
                term = cw_ref[k:k + 1, cols] * cxe_ref[:, first_tap + k:first_tap + k + l, :]
                conv = term if conv is None else conv + term
            conv = conv.reshape(tm, tn)
            st_ref[:, :, cols] = cxe_ref[:, l + first_tap:l + halo, :]
        else:
            cxe_ref[n, 0:halo, :] = carry_ref[n]
            cxe_ref[n, halo:, :] = cx
            carry_ref[n] = cxe_ref[n, tm:tm + halo, :]
            conv = None
            for k in range(SCONV_WIDTH):
                term = cw_ref[k:k + 1, cols] * cxe_ref[n, first_tap + k:first_tap + k + tm, :]
                conv = term if conv is None else conv + term
            st_ref[i // tiles_per_seq, n] = cxe_ref[n, tm + first_tap:tm + halo, :]
        gated_ref[:, cols] = (b_gate * conv).astype(BF16)

    gated = gated_ref[...]
    for n in range(nb):
        cols = slice(n * tn, (n + 1) * tn)
        o_ref[:, cols] = x_ref[:, cols] + _dot(gated, wout_ref[n])


def _sconv(x, g, w_in, conv_w, w_out, hist8, n_seq):
    m, d = x.shape
    sample = hist8 is not None
    nb, _, tn = w_out.shape
    hist_rows = SCONV_WIDTH - 1
    tm = SCONV_TOKEN_TILE
    l = m // n_seq
    tiles_per_seq = None if sample else l // tm
    bt = tm // l if sample else None

    def resident(shape):
        return pl.BlockSpec(shape, lambda i: (0,) * len(shape), pipeline_mode=pl.Buffered(1))

    in_specs = [
        pl.BlockSpec((tm, d), lambda i: (i, 0)),
        resident((1, d)),
        resident(w_in.shape),
        resident((SCONV_WIDTH, d)),
        resident(w_out.shape),
    ]
    args = [x, g, w_in, conv_w, w_out]
    if sample:
        in_specs.append(pl.BlockSpec((bt, SCONV_HALO, d), lambda i: (i, 0, 0)))
        args.append(hist8)
        st_spec = pl.BlockSpec((bt, hist_rows, d), lambda i: (i, 0, 0))
        scratch = [pltpu.VMEM((bt, l + SCONV_HALO, tn), F32), pltpu.VMEM((tm, d), BF16)]
    else:
        st_spec = pl.BlockSpec((n_seq, nb, hist_rows, tn), lambda i: (0, 0, 0, 0))
        scratch = [pltpu.VMEM((nb, tm + SCONV_HALO, tn), F32), pltpu.VMEM((tm, d), BF16),
                   pltpu.VMEM((nb, SCONV_HALO, tn), F32)]
    st_shape = (n_seq, hist_rows, d) if sample else (n_seq, nb, hist_rows, tn)
    out, state = pl.pallas_call(
        functools.partial(_sconv_kernel, sample=sample, tiles_per_seq=tiles_per_seq),
        grid=(m // tm,),
        in_specs=in_specs,
        out_specs=[pl.BlockSpec((tm, d), lambda i: (i, 0)), st_spec],
        out_shape=[jax.ShapeDtypeStruct((m, d), F32),
                   jax.ShapeDtypeStruct(st_shape, F32)],
        scratch_shapes=scratch,
        compiler_params=_params(1),
        name="sconv_sample" if sample else "sconv_prompt",
    )(*args)
    return out, _merge_col_blocks(state)


def _causal_taps_flat(glue_ref, n, w_ref, c0, tn, tm, halo, width):
    first_tap = halo - (width - 1)
    ext = tm + SUBLANES
    conv = None
    for r in range(SUBLANES):
        part = None
        for o in range(first_tap, first_tap + width):
            if o % SUBLANES != r:
                continue
            k = o - first_tap
            rows = tm if r == 0 else ext
            term = w_ref[k:k + 1, c0:c0 + tn] * glue_ref[n, o - r:o - r + rows, :]
            part = term if part is None else part + term
        if part is None:
            continue
        if r:
            part = pltpu.roll(part, ext - r, axis=0)[:tm]
        conv = part if conv is None else conv + part
    return conv


def _cconv_kernel(x_ref, g_ref, w1_ref, b1_ref, dw_ref, dwb_ref, lng_ref, lnb_ref, w2_ref, b2_ref,
                  *rest, sample, tiles_per_seq, nb):
    if sample:
        hist_ref, o_ref, st_ref, glue_ref, conv_ref = rest
    else:
        o_ref, st_ref, glue_ref, conv_ref, carry_ref = rest
    i = pl.program_id(0)
    tm, d = x_ref.shape
    tn = d // nb
    halo = CCONV_HALO
    first_tap = halo - (CCONV_WIDTH - 1)

    x = x_ref[...]
    h = _rms(x, g_ref[...]).astype(BF16)

    if not sample:
        @pl.when(i % tiles_per_seq == 0)
        def _():
            carry_ref[...] = jnp.zeros_like(carry_ref)

    for n in range(nb):
        c0 = n * tn
        a = _dot(h, w1_ref[n]) + b1_ref[:, c0:c0 + tn]
        gt = _dot(h, w1_ref[nb + n]) + b1_ref[:, d + c0:d + c0 + tn]
        glu = a * jax.nn.sigmoid(gt)
        if sample:
            bt, l = glue_ref.shape[0], glue_ref.shape[1] - halo
            glue_ref[:, 0:halo, :] = hist_ref[:, :, c0:c0 + tn]
            glue_ref[:, halo:, :] = glu.reshape(bt, l, tn)
            conv = None
            for k in range(CCONV_WIDTH):
                term = dw_ref[k:k + 1, c0:c0 + tn] * glue_ref[:, first_tap + k:first_tap + k + l, :]
                conv = term if conv is None else conv + term
            conv = conv.reshape(tm, tn)
            st_ref[:, :, c0:c0 + tn] = glue_ref[:, l + first_tap:l + halo, :]
        else:
            glue_ref[n, 0:halo, :] = carry_ref[n]
            glue_ref[n, halo:, :] = glu
            carry_ref[n] = glue_ref[n, tm:tm + halo, :]
            conv = _causal_taps_flat(glue_ref, n, dw_ref, c0, tn, tm, halo, CCONV_WIDTH)
            st_ref[i // tiles_per_seq, n] = glue_ref[n, tm + first_tap:tm + halo, :]
        conv_ref[:, c0:c0 + tn] = conv + dwb_ref[:, c0:c0 + tn]

    y = _layernorm(conv_ref[...], lng_ref[...], lnb_ref[...])
    z = (y * jax.nn.sigmoid(y)).astype(BF16)
    for n in range(nb):
        cols = slice(n * tn, (n + 1) * tn)
        o_ref[:, cols] = x_ref[:, cols] + _dot(z, w2_ref[n]) + b2_ref[:, cols]


def _cconv_pipelined_kernel(x_ref, xo_ref, g_ref, w1_ref, b1_ref, dw_ref, dwb_ref, lng_ref, lnb_ref, w2_ref,
                            b2_ref, o_ref, st_ref, glue_ref, conv_ref, z_ref, *, tiles_per_seq, nb):
    i = pl.program_id(0)
    n_tiles = pl.num_programs(0) - 2
    tm, d = x_ref.shape
    tn = d // nb
    halo = CCONV_HALO
    first_tap = halo - (CCONV_WIDTH - 1)
    cur = i % 2
    prv = 1 - cur

    @pl.when(i == 0)
    def _():
        glue_ref[1] = jnp.zeros(glue_ref.shape[1:], glue_ref.dtype)
        z_ref[...] = jnp.zeros_like(z_ref)

    z_done = z_ref[prv]
    h = _rms(x_ref[...], g_ref[...]).astype(BF16)
    seq_start = i % tiles_per_seq == 0
    for n in range(nb):
        c0 = n * tn
        cols = slice(c0, c0 + tn)
        a = _dot(h, w1_ref[n]) + b1_ref[:, cols]
        gt = _dot(h, w1_ref[nb + n]) + b1_ref[:, d + c0:d + c0 + tn]
        tail = glue_ref[prv, n, tm:tm + halo, :]
        glue_ref[cur, n, 0:halo, :] = jnp.where(seq_start, jnp.zeros_like(tail), tail)
        glue_ref[cur, n, halo:, :] = a * jax.nn.sigmoid(gt)
        conv = _causal_taps_flat(glue_ref.at[prv], n, dw_ref, c0, tn, tm, halo, CCONV_WIDTH)
        conv_ref[:, cols] = conv + dwb_ref[:, cols]
        o_ref[:, cols] = xo_ref[:, cols] + _dot(z_done, w2_ref[n]) + b2_ref[:, cols]
    y = _layernorm(conv_ref[...], lng_ref[...], lnb_ref[...])
    z_ref[cur] = (y * jax.nn.sigmoid(y)).astype(BF16)

    @pl.when(i < n_tiles)
    def _():
        for n in range(nb):
            st_ref[i // tiles_per_seq, n] = glue_ref[cur, n, tm + first_tap:tm + halo, :]


def _cast_kernel(w_ref, o_ref):
    o_ref[...] = w_ref[...].astype(o_ref.dtype)


def _col_blocks_bf16(w, tn):
    k, n = w.shape
    return pl.pallas_call(
        _cast_kernel,
        grid=(n // tn,),
        in_specs=[pl.BlockSpec((k, tn), lambda j: (0, j))],
        out_specs=pl.BlockSpec((None, k, tn), lambda j: (j, 0, 0)),
        out_shape=jax.ShapeDtypeStruct((n // tn, k, tn), BF16),
        compiler_params=_params(1),
        name="cast_col_blocks",
    )(w)


def _cconv(x, g, w1, b1, dw_w, dw_b, ln_g, ln_b, w2, b2, hist32, n_seq):
    m, d = x.shape
    sample = hist32 is not None
    nb, _, tn = w2.shape
    hist_rows = CCONV_WIDTH - 1
    tm = CCONV_TOKEN_TILE
    l = m // n_seq
    tiles_per_seq = None if sample else l // tm
    bt = tm // l if sample else None

    def resident(shape):
        return pl.BlockSpec(shape, lambda i: (0,) * len(shape), pipeline_mode=pl.Buffered(1))

    in_specs = [
        pl.BlockSpec((tm, d), lambda i: (i, 0)),
        resident((1, d)),
        resident((2 * nb, d, tn)),
        resident((1, 2 * d)),
        resident((CCONV_WIDTH, d)),
        resident((1, d)),
        resident((1, d)),
        resident((1, d)),
        resident((nb, d, tn)),
        resident((1, d)),
    ]
    args = [x, g, w1, b1, dw_w, dw_b, ln_g, ln_b, w2, b2]
    if sample:
        in_specs.append(pl.BlockSpec((bt, CCONV_HALO, d), lambda i: (i, 0, 0), pipeline_mode=pl.Buffered(1)))
        args.append(hist32)
        st_spec = pl.BlockSpec((bt, hist_rows, d), lambda i: (i, 0, 0))
        scratch = [pltpu.VMEM((bt, l + CCONV_HALO, tn), F32), pltpu.VMEM((tm, d), F32)]
    else:
        st_spec = pl.BlockSpec((n_seq, nb, hist_rows, tn), lambda i: (0, 0, 0, 0))
        scratch = [pltpu.VMEM((nb, tm + CCONV_HALO, tn), F32), pltpu.VMEM((tm, d), F32),
                   pltpu.VMEM((nb, CCONV_HALO, tn), F32)]
    st_shape = (n_seq, hist_rows, d) if sample else (n_seq, nb, hist_rows, tn)
    if not sample:
        n_tiles = m // tm

        def tile(offset):
            return lambda i: (jnp.clip(i - offset, 0, n_tiles - 1), 0)

        out, state = pl.pallas_call(
            functools.partial(_cconv_pipelined_kernel, tiles_per_seq=tiles_per_seq, nb=nb),
            grid=(n_tiles + 2,),
            in_specs=[pl.BlockSpec((tm, d), tile(0)), pl.BlockSpec((tm, d), tile(2))] + in_specs[1:],
            out_specs=[pl.BlockSpec((tm, d), tile(2)), st_spec],
            out_shape=[jax.ShapeDtypeStruct((m, d), F32),
                       jax.ShapeDtypeStruct(st_shape, F32)],
            scratch_shapes=[pltpu.VMEM((2, nb, tm + CCONV_HALO, tn), F32), pltpu.VMEM((tm, d), F32),
                            pltpu.VMEM((2, tm, d), BF16)],
            compiler_params=_params(1),
            name="cconv_prompt",
        )(x, *args)
        return out, _merge_col_blocks(state)
    out, state = pl.pallas_call(
        functools.partial(_cconv_kernel, sample=sample, tiles_per_seq=tiles_per_seq, nb=nb),
        grid=(m // tm,),
        in_specs=in_specs,
        out_specs=[pl.BlockSpec((tm, d), lambda i: (i, 0)), st_spec],
        out_shape=[jax.ShapeDtypeStruct((m, d), F32),
                   jax.ShapeDtypeStruct(st_shape, F32)],
        scratch_shapes=scratch,
        compiler_params=_params(1),
        name="cconv_sample" if sample else "cconv_prompt",
    )(*args)
    return out, _merge_col_blocks(state)


def _pad_front(hist, rows):
    return jnp.pad(hist, ((0, 0), (rows - hist.shape[1], 0), (0, 0)))


def _trunk(x, pool_hist, sconv_hist, cconv_hist, pos0, p, ffn_weights):
    b, s, d = x.shape
    sample = pool_hist is not None
    row = lambda v: v.reshape(1, -1)

    if sample:
        x, pool_state = _pool_sample(x, _pad_front(pool_hist, POOL_HALO), row(p['norm_mix_g'][0]),
                                     p['pool_w'], row(p['pool_scale']), pos0)
    else:
        x, pool_state = _pool_prompt(x, row(p['norm_mix_g'][0]), p['pool_w'], row(p['pool_scale']), pos0)
    x = x.reshape(b * s, d)
    used_weights = []

    def ffn(x, layer, g_final=None):
        if ffn_weights is None:
            w = (p['ffn_w_gate'], p['ffn_w_up'], p['ffn_w_down'])
        else:
            w = ffn_weights[layer]
        out, w_bf16 = _ffn(x, row(p['norm_ffn_g'][layer]), *w, layer, g_final)
        used_weights.append(w_bf16 or w)
        return out

    x = ffn(x, 0)

    if sample:
        reps = GMLP_CHUNK // s
        ws = jnp.tile(p['gmlp_w_s'][:, :s, :s], (1, reps, reps))
        bs = jnp.tile(p['gmlp_b_s'][:, :s], (1, reps))
        block_diag = s
    else:
        ws, bs, block_diag = p['gmlp_w_s'], p['gmlp_b_s'], 0
    x, v = _gmlp(x, row(p['norm_mix_g'][1]), p['gmlp_w_in'], row(p['gmlp_b_in']), row(p['gmlp_ln_g']),
                 row(p['gmlp_ln_b']), ws, bs[:, :, None], p['gmlp_w_out'], block_diag, emit_v=sample)
    x = ffn(x, 1)

    hist8 = _pad_front(sconv_hist, SCONV_HALO) if sample else None
    x, sconv_state = _sconv(x, row(p['norm_mix_g'][2]), p['sconv_w_in'], p['sconv_conv_w'],
                            p['sconv_w_out'], hist8, b)
    x = ffn(x, 2)

    hist32 = _pad_front(cconv_hist, CCONV_HALO) if sample else None
    x, cconv_state = _cconv(x, row(p['norm_mix_g'][3]), p['cconv_w_pw1'], row(p['cconv_b_pw1']),
                            p['cconv_dw_w'], row(p['cconv_dw_b']), row(p['cconv_ln_g']),
                            row(p['cconv_ln_b']), p['cconv_w_pw2'], row(p['cconv_b_pw2']), hist32, b)
    y = ffn(x, 3, row(p['norm_final_g']))
    gmlp_v = v.reshape(b, s, -1) if sample else None
    return y.reshape(b, s, d), pool_state, gmlp_v, sconv_state, cconv_state, used_weights


MIXER_MATMUL_WEIGHTS = ('pool_w',)
COL_BLOCKED_WEIGHTS = ('gmlp_w_in', 'gmlp_w_out', 'sconv_w_in', 'sconv_w_out', 'cconv_w_pw1', 'cconv_w_pw2')


def kernel(x_prompt, x_sample, state_pool, state_sconv, state_cconv, norm_mix_g, norm_ffn_g, norm_final_g, pool_w, pool_scale, gmlp_w_in, gmlp_b_in, gmlp_ln_g, gmlp_ln_b, gmlp_w_s, gmlp_b_s, gmlp_w_out, sconv_w_in, sconv_conv_w, sconv_w_out, cconv_w_pw1, cconv_b_pw1, cconv_dw_w, cconv_dw_b, cconv_ln_g, cconv_ln_b, cconv_w_pw2, cconv_b_pw2, ffn_w_gate, ffn_w_up, ffn_w_down):
    p = dict(norm_mix_g=norm_mix_g, norm_ffn_g=norm_ffn_g, norm_final_g=norm_final_g,
             pool_w=pool_w, pool_scale=pool_scale,
             gmlp_w_in=gmlp_w_in, gmlp_b_in=gmlp_b_in, gmlp_ln_g=gmlp_ln_g, gmlp_ln_b=gmlp_ln_b,
             gmlp_w_s=gmlp_w_s, gmlp_b_s=gmlp_b_s, gmlp_w_out=gmlp_w_out,
             sconv_w_in=sconv_w_in, sconv_conv_w=sconv_conv_w, sconv_w_out=sconv_w_out,
             cconv_w_pw1=cconv_w_pw1, cconv_b_pw1=cconv_b_pw1, cconv_dw_w=cconv_dw_w, cconv_dw_b=cconv_dw_b,
             cconv_ln_g=cconv_ln_g, cconv_ln_b=cconv_ln_b, cconv_w_pw2=cconv_w_pw2, cconv_b_pw2=cconv_b_pw2,
             ffn_w_gate=ffn_w_gate, ffn_w_up=ffn_w_up, ffn_w_down=ffn_w_down)
    for name in MIXER_MATMUL_WEIGHTS:
        p[name] = p[name].astype(BF16)
    for name in COL_BLOCKED_WEIGHTS:
        p[name] = _col_blocks_bf16(p[name], COL_TILE)
    y_s, pool_s, gmlp_v_s, sconv_s, cconv_s, ffn_weights = _trunk(
        x_sample, state_pool, state_sconv, state_cconv, PAST_LEN, p, None)
    y_p, pool_p, _, sconv_p, cconv_p, _ = _trunk(x_prompt, None, None, None, 0, p, ffn_weights)
    return (y_p, y_s, pool_p, pool_s, gmlp_v_s, sconv_p, sconv_s, cconv_p, cconv_s)
```
